```python
import math
import jax
import jax.numpy as jnp
from jax import lax
import numpy as np

D_MODEL = 2048
BATCH = 32
SEQ = 256
DEPTH = 4
DEC_BATCH = 8
DEC_SEQ = 4096
PAST_LEN = 256

GRID_W = 64
N_HEADS = 12
QK_DIM = 64
V_DIM = 128
ATTN_W = N_HEADS * V_DIM
SSM_CH = 16
SSM_W = 512
SSM_GROUPS = SSM_W // SSM_CH
SSM_STATE = 64
MIX_W = ATTN_W + SSM_W
QK_W = N_HEADS * 2 * QK_DIM
IN_W = 2 * QK_W + ATTN_W + SSM_W
D_FF = 5632
N_EXPERTS = 8
TOP_K = 2
MOE_BLOCK = 256
Q_BLOCK = 128
ROPE_BASE = 10000.0
EPS = 1e-6
N_DENSE = (DEPTH + 1) // 2
N_MOE = DEPTH // 2

kernel_name = 'hybrid_diffattn_s5_prefix_dit_step'

F32 = jnp.float32


def rmsnorm(x, g):
    xf = x.astype(F32)
    y = xf * lax.rsqrt(jnp.mean(xf * xf, axis=-1, keepdims=True) + EPS)
    return (y * g.astype(F32)).astype(x.dtype)


def lambda_init(l):
    return 0.8 - 0.6 * math.exp(-0.3 * l)


def adaln(cond, w_mod, b_mod):
    m = (jax.nn.silu(cond) @ w_mod + b_mod)[..., None, :]
    return jnp.split(m, 6, axis=-1)


def _rotate(x, pos):
    n = x.shape[-1]
    inv = 1.0 / (ROPE_BASE ** (jnp.arange(0, n, 2, dtype=F32) / n))
    ang = pos.astype(F32)[:, None] * inv[None, :]
    cos = jnp.concatenate([jnp.cos(ang), jnp.cos(ang)], -1)[:, None, None, :]
    sin = jnp.concatenate([jnp.sin(ang), jnp.sin(ang)], -1)[:, None, None, :]
    x1, x2 = jnp.split(x, 2, axis=-1)
    return x * cos + jnp.concatenate([-x2, x1], -1) * sin


def rope_2d(x, rows):
    row = jnp.repeat(jnp.arange(rows), GRID_W)
    col = jnp.tile(jnp.arange(GRID_W), rows)
    xf = x.astype(F32)
    half = QK_DIM // 2
    out = jnp.concatenate([_rotate(xf[..., :half], row), _rotate(xf[..., half:], col)], -1)
    return out.astype(x.dtype)


def split_proj(z):
    B, L, _ = z.shape
    q = z[..., :QK_W].reshape(B, L, N_HEADS, 2, QK_DIM)
    k = z[..., QK_W:2 * QK_W].reshape(B, L, N_HEADS, 2, QK_DIM)
    v = z[..., 2 * QK_W:2 * QK_W + ATTN_W].reshape(B, L, N_HEADS, V_DIM)
    u = z[..., 2 * QK_W + ATTN_W:]
    return q, k, v, u


def diff_lambda(p, l):
    lq1, lk1 = p['lam_q1'].astype(F32), p['lam_k1'].astype(F32)
    lq2, lk2 = p['lam_q2'].astype(F32), p['lam_k2'].astype(F32)
    return jnp.exp(jnp.sum(lq1 * lk1)) - jnp.exp(jnp.sum(lq2 * lk2)) + lambda_init(l)


def diff_softmax_values(q, keys, vals, lam):
    s = jnp.einsum('bqhmd,bkhmd->bhmqk', q, keys, preferred_element_type=F32) * (QK_DIM ** -0.5)
    prob = jax.nn.softmax(s, axis=-1)
    w = prob[:, :, 0] - lam * prob[:, :, 1]
    return jnp.einsum('bhqk,bkhe->bqhe', w.astype(vals.dtype), vals)


def blocked_diff_attention(q, keys, vals, lam):
    B, L = q.shape[:2]
    nb = L // Q_BLOCK
    qb = jnp.moveaxis(q.reshape(B, nb, Q_BLOCK, N_HEADS, 2, QK_DIM), 1, 0)
    ob = lax.map(lambda qq: diff_softmax_values(qq, keys, vals, lam), qb)
    return jnp.moveaxis(ob, 0, 1).reshape(B, L, N_HEADS, V_DIM)


def diff_head_out(o, head_g, l):
    B, L = o.shape[:2]
    o = rmsnorm(o, head_g) * (1.0 - lambda_init(l))
    return o.reshape(B, L, ATTN_W)


def _s5_combine(e1, e2):
    a1, b1 = e1
    a2, b2 = e2
    return a2 * a1, a2 * b1 + b2


def s5_mixer(u, p, h0):
    B, L, _ = u.shape
    uf = u.astype(F32).reshape(B, L, SSM_GROUPS, SSM_CH)
    y = p['ssm_d'].astype(F32).reshape(SSM_GROUPS, SSM_CH) * uf
    finals = []
    for d, reverse in enumerate((False, True)):
        lam = lax.complex(p['ssm_a_re'][d].astype(F32), p['ssm_a_im'][d].astype(F32))
        lam_dt = lam * jnp.exp(p['ssm_log_dt'][d].astype(F32))[:, None]
        a_bar = jnp.exp(lam_dt)
        b_bar = ((a_bar - 1.0) / lam)[..., None] * lax.complex(p['ssm_b_re'][d].astype(F32), p['ssm_b_im'][d].astype(F32))
        c_mat = lax.complex(p['ssm_c_re'][d].astype(F32), p['ssm_c_im'][d].astype(F32))
        bu = jnp.einsum('blgc,gpc->blgp', uf.astype(jnp.complex64), b_bar)
        a = jnp.broadcast_to(a_bar, (1, L) + a_bar.shape)
        _, h = lax.associative_scan(_s5_combine, (a, bu), axis=1, reverse=reverse)
        if h0 is None:
            finals.append(h[:, 0] if reverse else h[:, -1])
        else:
            steps = (jnp.arange(L, 0, -1) if reverse else jnp.arange(1, L + 1)).astype(F32)
            h = h + jnp.exp(steps[:, None, None] * lam_dt)[None] * h0[:, d, None]
        y = y + jnp.einsum('blgp,gcp->blgc', h, c_mat).real
    y = jax.nn.gelu(y.reshape(B, L, SSM_W))
    y = y * jax.nn.sigmoid(y @ p['w_glu'].astype(F32))
    y = rmsnorm(y, p['ssm_norm_g']).astype(u.dtype)
    return y, (jnp.stack(finals, axis=1) if h0 is None else None)


def swiglu(x, w1, w3, w2):
    return (jax.nn.silu(x @ w1) * (x @ w3)) @ w2


def moe_swiglu(h, w_router, w1, w3, w2):
    B, L, D = h.shape
    x2 = h.reshape(-1, D)
    T = x2.shape[0]
    logits = jnp.einsum('td,de->te', x2, w_router, preferred_element_type=F32)
    top_logit, top_e = lax.top_k(logits, TOP_K)
    gate = jax.nn.softmax(top_logit, axis=-1)
    n_assign = T * TOP_K
    e_flat = top_e.reshape(-1)
    tok_flat = jnp.arange(n_assign, dtype=jnp.int32) // TOP_K
    order = jnp.argsort(e_flat)
    e_sorted = e_flat[order]
    counts = jnp.bincount(e_flat, length=N_EXPERTS)
    padded = (counts + MOE_BLOCK - 1) // MOE_BLOCK * MOE_BLOCK
    pad_end = jnp.cumsum(padded)
    pad_start = pad_end - padded
    start = jnp.cumsum(counts) - counts
    slot = pad_start[e_sorted] + jnp.arange(n_assign, dtype=jnp.int32) - start[e_sorted]
    n_blocks = -(-n_assign // MOE_BLOCK) + N_EXPERTS
    n_slots = n_blocks * MOE_BLOCK
    slot_tok = jnp.zeros((n_slots,), jnp.int32).at[slot].set(tok_flat[order])
    slot_gate = jnp.zeros((n_slots,), F32).at[slot].set(gate.reshape(-1)[order])
    block_e = jnp.minimum(jnp.searchsorted(pad_end, jnp.arange(n_blocks) * MOE_BLOCK, side='right'), N_EXPERTS - 1)
    xs = x2[slot_tok].reshape(n_blocks, MOE_BLOCK, D)

    def expert_block(args):
        xb, e = args
        return swiglu(xb, w1[e], w3[e], w2[e])

    ys = lax.map(expert_block, (xs, block_e)).reshape(n_slots, D)
    out = jnp.zeros_like(x2).at[slot_tok].add(ys * slot_gate[:, None].astype(ys.dtype))
    return out.reshape(B, L, D)


def trunk_layer(x, cond, l, p, ctx=None, rows=None):
    shift1, scale1, gate1, shift2, scale2, gate2 = adaln(cond, p['w_mod'], p['b_mod'])
    h = rmsnorm(x, p['norm1_g']) * (1 + scale1) + shift1
    q, k, v, u = split_proj(h @ p['w_in'])
    lam = diff_lambda(p, l)
    if ctx is None:
        keys, vals, h0 = k, v, None
    else:
        ck, cv, h0 = ctx
        q = rope_2d(q, rows)
        k = rope_2d(k, rows)
        keys = jnp.concatenate([ck.astype(k.dtype), k], axis=1)
        vals = jnp.concatenate([cv.astype(v.dtype), v], axis=1)
    attn = diff_head_out(blocked_diff_attention(q, keys, vals, lam), p['head_g'], l)
    ssm, finals = s5_mixer(u, p, h0)
    x = x + gate1 * (jnp.concatenate([attn, ssm.astype(attn.dtype)], axis=-1) @ p['w_out'])
    h = rmsnorm(x, p['norm2_g']) * (1 + scale2) + shift2
    if l % 2 == 0:
        f = swiglu(h, p['ffn_w1'], p['ffn_w3'], p['ffn_w2'])
    else:
        f = moe_swiglu(h, p['moe_router'], p['moe_w1'], p['moe_w3'], p['moe_w2'])
    x = x + gate2 * f
    return x, (k, v, finals)


def setup_inputs(seed: int = 0) -> dict:
    key = jax.random.key(seed)
    ks = iter(jax.random.split(key, 48))

    def nrm(shape, scale=1.0):
        return scale * jax.random.normal(next(ks), shape, F32)

    def gain(shape):
        return 1.0 + 0.02 * jax.random.normal(next(ks), shape, F32)

    G, P = SSM_GROUPS, SSM_STATE
    inputs = {}
    inputs['x_prompt'] = nrm((BATCH, SEQ, D_MODEL))
    inputs['x_sample'] = nrm((DEC_BATCH, DEC_SEQ, D_MODEL))
    inputs['cache_k'] = nrm((DEC_BATCH, DEPTH, PAST_LEN, N_HEADS, 2, QK_DIM))
    inputs['cache_v'] = nrm((DEC_BATCH, DEPTH, PAST_LEN, N_HEADS, V_DIM))
    inputs['state_ssm_re'] = nrm((DEC_BATCH, DEPTH, 2, G, P), 0.3)
    inputs['state_ssm_im'] = nrm((DEC_BATCH, DEPTH, 2, G, P), 0.3)
    inputs['c'] = nrm((DEC_BATCH, D_MODEL))
    inputs['c_ctx'] = nrm((D_MODEL,))
    inputs['w_mod'] = nrm((DEPTH, D_MODEL, 6 * D_MODEL), 0.5 * D_MODEL ** -0.5)
    inputs['b_mod'] = nrm((DEPTH, 6 * D_MODEL), 0.01)
    inputs['norm1_g'] = gain((DEPTH, D_MODEL))
    inputs['norm2_g'] = gain((DEPTH, D_MODEL))
    inputs['w_in'] = nrm((DEPTH, D_MODEL, IN_W), D_MODEL ** -0.5)
    inputs['w_out'] = nrm((DEPTH, MIX_W, D_MODEL), MIX_W ** -0.5)
    inputs['lam_q1'] = nrm((DEPTH, QK_DIM), 0.1)
    inputs['lam_k1'] = nrm((DEPTH, QK_DIM), 0.1)
    inputs['lam_q2'] = nrm((DEPTH, QK_DIM), 0.1)
    inputs['lam_k2'] = nrm((DEPTH, QK_DIM), 0.1)
    inputs['head_g'] = gain((DEPTH, V_DIM))
    inputs['ssm_a_re'] = -0.5 + nrm((DEPTH, 2, G, P), 0.01)
    inputs['ssm_a_im'] = math.pi * jnp.arange(P, dtype=F32) + nrm((DEPTH, 2, G, P), 0.01)
    inputs['ssm_log_dt'] = jax.random.uniform(next(ks), (DEPTH, 2, G), F32, math.log(1e-3), math.log(1e-1))
    inputs['ssm_b_re'] = nrm((DEPTH, 2, G, P, SSM_CH), (0.5 / SSM_CH) ** 0.5)
    inputs['ssm_b_im'] = nrm((DEPTH, 2, G, P, SSM_CH), (0.5 / SSM_CH) ** 0.5)
    inputs['ssm_c_re'] = nrm((DEPTH, 2, G, SSM_CH, P), (0.5 / P) ** 0.5)
    inputs['ssm_c_im'] = nrm((DEPTH, 2, G, SSM_CH, P), (0.5 / P) ** 0.5)
    inputs['ssm_d'] = nrm((DEPTH, SSM_W))
    inputs['w_glu'] = nrm((DEPTH, SSM_W, SSM_W), SSM_W ** -0.5)
    inputs['ssm_norm_g'] = gain((DEPTH, SSM_W))
    inputs['ffn_w1'] = nrm((N_DENSE, D_MODEL, D_FF), D_MODEL ** -0.5)
    inputs['ffn_w3'] = nrm((N_DENSE, D_MODEL, D_FF), D_MODEL ** -0.5)
    inputs['ffn_w2'] = nrm((N_DENSE, D_FF, D_MODEL), D_FF ** -0.5)
    inputs['moe_router'] = nrm((N_MOE, D_MODEL, N_EXPERTS), D_MODEL ** -0.5)
    inputs['moe_w1'] = nrm((N_MOE, N_EXPERTS, D_MODEL, D_FF), D_MODEL ** -0.5)
    inputs['moe_w3'] = nrm((N_MOE, N_EXPERTS, D_MODEL, D_FF), D_MODEL ** -0.5)
    inputs['moe_w2'] = nrm((N_MOE, N_EXPERTS, D_FF, D_MODEL), D_FF ** -0.5)
    inputs['final_g'] = gain((D_MODEL,))
    return inputs


def reference(x_prompt, x_sample, cache_k, cache_v, state_ssm_re, state_ssm_im, c, c_ctx,
              w_mod, b_mod, norm1_g, norm2_g, w_in, w_out, lam_q1, lam_k1, lam_q2, lam_k2, head_g,
              ssm_a_re, ssm_a_im, ssm_log_dt, ssm_b_re, ssm_b_im, ssm_c_re, ssm_c_im, ssm_d, w_glu,
              ssm_norm_g, ffn_w1, ffn_w3, ffn_w2, moe_router, moe_w1, moe_w3, moe_w2, final_g):
    rows = x_sample.shape[1] // GRID_W
    ctx_states = lax.complex(state_ssm_re.astype(F32), state_ssm_im.astype(F32))

    def layer_params(l):
        p = dict(w_mod=w_mod[l], b_mod=b_mod[l], norm1_g=norm1_g[l], norm2_g=norm2_g[l],
                 w_in=w_in[l], w_out=w_out[l], lam_q1=lam_q1[l], lam_k1=lam_k1[l],
                 lam_q2=lam_q2[l], lam_k2=lam_k2[l], head_g=head_g[l],
                 ssm_a_re=ssm_a_re[l], ssm_a_im=ssm_a_im[l], ssm_log_dt=ssm_log_dt[l],
                 ssm_b_re=ssm_b_re[l], ssm_b_im=ssm_b_im[l], ssm_c_re=ssm_c_re[l], ssm_c_im=ssm_c_im[l],
                 ssm_d=ssm_d[l], w_glu=w_glu[l], ssm_norm_g=ssm_norm_g[l])
        if l % 2 == 0:
            p.update(ffn_w1=ffn_w1[l // 2], ffn_w3=ffn_w3[l // 2], ffn_w2=ffn_w2[l // 2])
        else:
            p.update(moe_router=moe_router[l // 2], moe_w1=moe_w1[l // 2],
                     moe_w3=moe_w3[l // 2], moe_w2=moe_w2[l // 2])
        return p

    yp = x_prompt
    ys = x_sample
    ks, vs, sts = [], [], []
    for l in range(DEPTH):
        p = layer_params(l)
        yp, (k_ctx, v_ctx, fin) = trunk_layer(yp, c_ctx, l, p)
        ks.append(k_ctx)
        vs.append(v_ctx)
        sts.append(fin)
        ys, _ = trunk_layer(ys, c, l, p, ctx=(cache_k[:, l], cache_v[:, l], ctx_states[:, l]), rows=rows)
    y_prompt = rmsnorm(yp, final_g)
    y_sample = rmsnorm(ys, final_g)
    new_cache_k = jnp.stack(ks, axis=1)
    new_cache_v = jnp.stack(vs, axis=1)
    st = jnp.stack(sts, axis=1)
    new_state_re = st.real.astype(x_prompt.dtype)
    new_state_im = st.imag.astype(x_prompt.dtype)
    return (y_prompt, y_sample, new_cache_k, new_cache_v, new_state_re, new_state_im)
```

```python
import functools
import math

import jax
import jax.numpy as jnp
from jax import lax
from jax.experimental import pallas as pl
from jax.experimental.pallas import tpu as pltpu

F32 = jnp.float32
BF16 = jnp.bfloat16

EPS = 1e-6
GRID_W = 64
ROPE_BASE = 10000.0
TOP_K = 2
SSM_CHUNK = 16
MOE_ROWS = 512
N_MOD = 6
MOD_ROWS = 8
LANES = 128
VMEM_LIMIT_BYTES = 56 * 1024 * 1024


def _lambda_init(l):
    return 0.8 - 0.6 * math.exp(-0.3 * l)


def _tile(dim, pref):
    if dim <= pref:
        return dim
    t = pref
    while dim % t:
        t -= 8
    assert t > 0
    return t


def _params(*sem):
    return pltpu.CompilerParams(dimension_semantics=sem, vmem_limit_bytes=VMEM_LIMIT_BYTES)


def _sigmoid(x):
    return 1.0 / (1.0 + jnp.exp(-x))


def _norm_mod(x, g, mod, shift_row, scale_row):
    y = x * lax.rsqrt(jnp.mean(x * x, axis=-1, keepdims=True) + EPS) * g
    return y * (1.0 + mod[scale_row:scale_row + 1, :]) + mod[shift_row:shift_row + 1, :]


def _adaln_kernel(c_ref, w_ref, b_ref, o_ref):
    c = c_ref[...]
    s = (c * _sigmoid(c)).astype(BF16)
    o_ref[...] = jnp.dot(s, w_ref[...].astype(BF16), preferred_element_type=F32) + b_ref[...]


def _adaln(cond, w_mod, b_mod):
    depth, d, n = w_mod.shape
    r = cond.shape[0]
    tn = _tile(n, 1024)
    return pl.pallas_call(
        _adaln_kernel,
        out_shape=jax.ShapeDtypeStruct((depth, r, n), F32),
        grid=(depth, n // tn),
        in_specs=[
            pl.BlockSpec((r, d), lambda l, j: (0, 0)),
            pl.BlockSpec((None, d, tn), lambda l, j: (l, 0, j)),
            pl.BlockSpec((None, 1, tn), lambda l, j: (l, 0, j)),
        ],
        out_specs=pl.BlockSpec((None, r, tn), lambda l, j: (l, 0, j)),
        compiler_params=_params("arbitrary", "arbitrary"),
        name="adaln",
    )(cond, w_mod, b_mod.reshape(depth, 1, n))


def _in_proj_kernel(x_ref, g_ref, mod_ref, w_ref, o_ref, h_scr):
    @pl.when(pl.program_id(1) == 0)
    def _():
        h_scr[...] = _norm_mod(x_ref[...], g_ref[...], mod_ref[...], 0, 1).astype(BF16)

    o_ref[...] = jnp.dot(h_scr[...], w_ref[...], preferred_element_type=F32)


def _in_proj(x, g, mod, w, rows_per_mod):
    t, d = x.shape
    n = w.shape[1]
    tm = _tile(rows_per_mod, 1024)
    tn = _tile(n, 1024)
    tiles_per_mod = rows_per_mod // tm
    return pl.pallas_call(
        _in_proj_kernel,
        out_shape=jax.ShapeDtypeStruct((t, n), F32),
        grid=(t // tm, n // tn),
        in_specs=[
            pl.BlockSpec((tm, d), lambda i, j: (i, 0)),
            pl.BlockSpec((1, d), lambda i, j: (0, 0)),
            pl.BlockSpec((None, MOD_ROWS, d), lambda i, j: (i // tiles_per_mod, 0, 0)),
            pl.BlockSpec((d, tn), lambda i, j: (0, j)),
        ],
        out_specs=pl.BlockSpec((tm, tn), lambda i, j: (i, j)),
        scratch_shapes=[pltpu.VMEM((tm, d), BF16)],
        compiler_params=_params("arbitrary", "arbitrary"),
        name="in_proj",
    )(x, g.reshape(1, d), mod, w)


def _rope(x, cos, sin_signed):
    lane = lax.broadcasted_iota(jnp.int32, x.shape, 1)
    first_half = (lane % 32) < 16
    partner = jnp.where(first_half, pltpu.roll(x, LANES - 16, 1), pltpu.roll(x, 16, 1))
    return x * cos + partner * sin_signed


def _attn_kernel(*refs, n_past, rope, qk_dim, out_scale):
    if rope:
        (lam_ref, q_ref, k_ref, v_ref, ck_ref, cv_ref, cq_ref, sq_ref, ckk_ref, skk_ref, hg_ref,
         o_ref, k0_scr, k1_scr, v_scr) = refs
    else:
        lam_ref, q_ref, k_ref, v_ref, hg_ref, o_ref, k0_scr, k1_scr, v_scr = refs

    @pl.when(pl.program_id(2) == 0)
    def _():
        k = k_ref[...]
        if rope:
            k = _rope(k, ckk_ref[...], skk_ref[...])
        kb = k.astype(BF16)
        if n_past:
            ck = ck_ref[...].astype(BF16)
            k0_scr[0:n_past, :] = ck[:, :qk_dim]
            k1_scr[0:n_past, :] = ck[:, qk_dim:]
            v_scr[0:n_past, :] = cv_ref[...].astype(BF16)
        k0_scr[n_past:, :] = kb[:, :qk_dim]
        k1_scr[n_past:, :] = kb[:, qk_dim:]
        v_scr[n_past:, :] = v_ref[...].astype(BF16)

    q = q_ref[...]
    if rope:
        q = _rope(q, cq_ref[...], sq_ref[...])
    qb = (q * (qk_dim ** -0.5)).astype(BF16)
    nt = (((1,), (1,)), ((), ()))

    def probs(qh, k_scr):
        s = lax.dot_general(qh, k_scr[...], nt, preferred_element_type=F32)
        p = jnp.exp(s - jnp.max(s, axis=-1, keepdims=True))
        return p * (1.0 / jnp.sum(p, axis=-1, keepdims=True))

    w = probs(qb[:, :qk_dim], k0_scr) - lam_ref[0] * probs(qb[:, qk_dim:], k1_scr)
    o = jnp.dot(w.astype(BF16), v_scr[...], preferred_element_type=F32)
    o = o * lax.rsqrt(jnp.mean(o * o, axis=-1, keepdims=True) + EPS) * hg_ref[...]
    o_ref[...] = (o * out_scale).astype(o_ref.dtype)


def _attention(z, lam, head_g, layer, batch, seq, n_heads, qk_dim, v_dim, past=None, rope_tabs=None):
    assert 2 * qk_dim == LANES and v_dim == LANES
    t = batch * seq
    tq = _tile(seq, 256)
    nq = seq // tq
    rope = past is not None
    n_past = past[0].shape[2] if rope else 0
    nk = n_past + seq
    q_spec = pl.BlockSpec((tq, LANES), lambda b, h, i: (b * nq + i, h))
    k_spec = pl.BlockSpec((seq, LANES), lambda b, h, i: (b, n_heads + h))
    v_spec = pl.BlockSpec((seq, LANES), lambda b, h, i: (b, 2 * n_heads + h))
    smem = pl.BlockSpec(memory_space=pltpu.SMEM)
    hg_spec = pl.BlockSpec((1, LANES), lambda b, h, i: (0, 0))
    if rope:
        ck, cv = past
        cos, sin = rope_tabs
        in_specs = [
            smem, q_spec, k_spec, v_spec,
            pl.BlockSpec((None, None, n_past, LANES), lambda b, h, i: (b, layer, 0, h)),
            pl.BlockSpec((None, None, n_past, LANES), lambda b, h, i: (b, layer, 0, h)),
            pl.BlockSpec((tq, LANES), lambda b, h, i: (i, 0)),
            pl.BlockSpec((tq, LANES), lambda b, h, i: (i, 0)),
            pl.BlockSpec((seq, LANES), lambda b, h, i: (0, 0)),
            pl.BlockSpec((seq, LANES), lambda b, h, i: (0, 0)),
            hg_spec,
        ]
        args = (lam, z, z, z, ck, cv, cos, sin, cos, sin, head_g.reshape(1, LANES))
    else:
        in_specs = [smem, q_spec, k_spec, v_spec, hg_spec]
        args = (lam, z, z, z, head_g.reshape(1, LANES))
    return pl.pallas_call(
        functools.partial(_attn_kernel, n_past=n_past, rope=rope, qk_dim=qk_dim,
                          out_scale=1.0 - _lambda_init(layer)),
        out_shape=jax.ShapeDtypeStruct((t, n_heads * v_dim), BF16),
        grid=(batch, n_heads, nq),
        in_specs=in_specs,
        out_specs=pl.BlockSpec((tq, LANES), lambda b, h, i: (b * nq + i, h)),
        scratch_shapes=[pltpu.VMEM((nk, qk_dim), BF16), pltpu.VMEM((nk, qk_dim), BF16),
                        pltpu.VMEM((nk, v_dim), BF16)],
        compiler_params=_params("arbitrary", "arbitrary", "arbitrary"),
        name="diff_attn",
    )(*args)


def _rope_tables(seq, qk_dim):
    half = qk_dim // 2
    inv = 1.0 / (ROPE_BASE ** (jnp.arange(0, half, 2, dtype=F32) / half))
    pos = jnp.arange(seq)
    sign = jnp.concatenate([-jnp.ones((half // 2,), F32), jnp.ones((half // 2,), F32)])

    def tab(p):
        ang = p.astype(F32)[:, None] * inv[None, :]
        cos = jnp.concatenate([jnp.cos(ang), jnp.cos(ang)], -1)
        sin = jnp.concatenate([jnp.sin(ang), jnp.sin(ang)], -1) * sign[None, :]
        return cos, sin

    cr, sr = tab(pos // GRID_W)
    cc, sc = tab(pos % GRID_W)
    cos = jnp.concatenate([cr, cc], -1)
    sin = jnp.concatenate([sr, sc], -1)
    reps = LANES // qk_dim
    return jnp.tile(cos, (1, reps)), jnp.tile(sin, (1, reps))


def _ssm_kernel(x_ref, m_ref, bre_ref, bim_ref, cre_ref, cim_ref, are_ref, aim_ref, h0re_ref, h0im_ref,
                y_ref, fre_ref, fim_ref, sre_scr, sim_scr, hfre_scr, hfim_scr, hbre_scr, hbim_scr,
                *, n_chunks, batch, state):
    xb = x_ref[...].astype(BF16)
    sre_scr[...] = jnp.dot(xb, bre_ref[...], preferred_element_type=F32)
    sim_scr[...] = jnp.dot(xb, bim_ref[...], preferred_element_type=F32)
    ar = are_ref[...]
    ai = aim_ref[...]
    fwd_lane = lax.broadcasted_iota(jnp.int32, (batch, 2 * state), 1) < state

    def step(i, carry):
        hre, him = carry
        rf = pl.ds(pl.multiple_of(i * batch, batch), batch)
        rb = pl.ds(pl.multiple_of((n_chunks - 1 - i) * batch, batch), batch)
        hfre_scr[rf, :] = hre
        hfim_scr[rf, :] = him
        hbre_scr[rb, :] = hre
        hbim_scr[rb, :] = him
        sre = jnp.where(fwd_lane, sre_scr[rf, :], sre_scr[rb, :])
        sim = jnp.where(fwd_lane, sim_scr[rf, :], sim_scr[rb, :])
        return ar * hre - ai * him + sre, ar * him + ai * hre + sim

    hre, him = lax.fori_loop(0, n_chunks, step, (h0re_ref[...], h0im_ref[...]))
    fre_ref[...] = hre
    fim_ref[...] = him
    lane = lax.broadcasted_iota(jnp.int32, hfre_scr.shape, 1) < state
    hs_re = jnp.where(lane, hfre_scr[...], hbre_scr[...]).astype(BF16)
    hs_im = jnp.where(lane, hfim_scr[...], hbim_scr[...]).astype(BF16)
    y = jnp.dot(xb, m_ref[...], preferred_element_type=F32)
    y += jnp.dot(hs_re, cre_ref[...], preferred_element_type=F32)
    y += jnp.dot(hs_im, cim_ref[...], preferred_element_type=F32)
    y_ref[...] = y


def _ssm_matrices(a_re, a_im, log_dt, b_re, b_im, c_re, c_im):
    tc = SSM_CHUNK
    hi = lax.Precision.HIGHEST
    g, p = a_re.shape[1:]
    ch = b_re.shape[-1]
    tau = jnp.arange(tc + 1, dtype=F32)
    s_idx = jnp.arange(tc)
    m_tot = 0.0
    b_parts, c_parts, a_parts = [], [], []
    for d in range(2):
        lam = lax.complex(a_re[d].astype(F32), a_im[d].astype(F32))
        lam_dt = lam * jnp.exp(log_dt[d].astype(F32))[:, None]
        a_bar = jnp.exp(lam_dt)
        b_bar = ((a_bar - 1.0) / lam)[..., None] * lax.complex(b_re[d].astype(F32), b_im[d].astype(F32))
        c_mat = lax.complex(c_re[d].astype(F32), c_im[d].astype(F32))
        pw = jnp.exp(tau[:, None, None] * lam_dt[None])
        kern = jnp.einsum('gcp,tgp,gpd->tgcd', c_mat, pw[:tc], b_bar, precision=hi).real
        lag = (s_idx[None, :] - s_idx[:, None]) if d == 0 else (s_idx[:, None] - s_idx[None, :])
        valid = (lag >= 0)
        kk = kern[jnp.clip(lag, 0, tc - 1)]
        kk = jnp.where(valid[:, :, None, None, None], kk, 0.0)
        m_tot = m_tot + jnp.transpose(kk, (2, 0, 4, 1, 3)).reshape(g, tc * ch, tc * ch)
        pw_in = pw[tc - 1 - s_idx] if d == 0 else pw[s_idx]
        bs = pw_in[:, :, :, None] * b_bar[None]
        b_parts.append(jnp.transpose(bs, (1, 0, 3, 2)).reshape(g, tc * ch, p))
        pw_out = pw[s_idx + 1] if d == 0 else pw[tc - s_idx]
        cs = c_mat[None] * pw_out[:, :, None, :]
        c_parts.append(jnp.transpose(cs, (1, 3, 0, 2)).reshape(g, p, tc * ch))
        a_parts.append(pw[tc])
    bcat = jnp.concatenate(b_parts, axis=-1)
    ccat = jnp.concatenate(c_parts, axis=1)
    acat = jnp.concatenate(a_parts, axis=-1)[:, None, :]
    return (m_tot.astype(BF16), bcat.real.astype(BF16), bcat.imag.astype(BF16),
            ccat.real.astype(BF16), (-ccat.imag).astype(BF16), acat.real, acat.imag)


def _ssm(u, mats, h0re, h0im, batch, seq):
    m, bre, bim, cre, cim, are, aim = mats
    g, kc, _ = m.shape
    ch = kc // SSM_CHUNK
    p2 = bre.shape[-1]
    nc = seq // SSM_CHUNK
    n = nc * batch
    x = u.reshape(batch, nc, SSM_CHUNK, g, ch).transpose(3, 1, 0, 2, 4).reshape(g, n, kc)
    grp = lambda shape: pl.BlockSpec((None,) + shape, lambda i: (i,) + (0,) * len(shape))
    y, fre, fim = pl.pallas_call(
        functools.partial(_ssm_kernel, n_chunks=nc, batch=batch, state=p2 // 2),
        out_shape=(jax.ShapeDtypeStruct((g, n, kc), F32),
                   jax.ShapeDtypeStruct((g, batch, p2), F32),
                   jax.ShapeDtypeStruct((g, batch, p2), F32)),
        grid=(g,),
        in_specs=[grp((n, kc)), grp((kc, kc)), grp((kc, p2)), grp((kc, p2)), grp((p2, kc)), grp((p2, kc)),
                  grp((1, p2)), grp((1, p2)), grp((batch, p2)), grp((batch, p2))],
        out_specs=(grp((n, kc)), grp((batch, p2)), grp((batch, p2))),
        scratch_shapes=[pltpu.VMEM((n, p2), F32)] * 6,
        compiler_params=_params("arbitrary"),
        name="s5_scan",
    )(x, m, bre, bim, cre, cim, are, aim, h0re, h0im)
    y = y.reshape(g, nc, batch, SSM_CHUNK, ch).transpose(2, 1, 3, 0, 4).reshape(batch * seq, g * ch)
    return y, fre, fim


def _ssm_post_kernel(y_ref, u_ref, d_ref, w_ref, g_ref, o_ref):
    y = y_ref[...] + d_ref[...] * u_ref[...]
    y = 0.5 * y * (1.0 + jnp.tanh(math.sqrt(2.0 / math.pi) * (y + 0.044715 * (y * y * y))))
    y = y * _sigmoid(jnp.dot(y.astype(BF16), w_ref[...], preferred_element_type=F32))
    y = y * lax.rsqrt(jnp.mean(y * y, axis=-1, keepdims=True) + EPS) * g_ref[...]
    o_ref[...] = y.astype(o_ref.dtype)


def _ssm_post(y, z, u_col_block, ssm_d, w_glu, norm_g):
    t, w = y.shape
    tm = _tile(t, 1024)
    row = lambda i: (i, 0)
    fixed = lambda i: (0, 0)
    return pl.pallas_call(
        _ssm_post_kernel,
        out_shape=jax.ShapeDtypeStruct((t, w), BF16),
        grid=(t // tm,),
        in_specs=[pl.BlockSpec((tm, w), row), pl.BlockSpec((tm, w), lambda i: (i, u_col_block)),
                  pl.BlockSpec((1, w), fixed), pl.BlockSpec((w, w), fixed), pl.BlockSpec((1, w), fixed)],
        out_specs=pl.BlockSpec((tm, w), row),
        compiler_params=_params("arbitrary"),
        name="s5_post",
    )(y, z, ssm_d.reshape(1, w), w_glu, norm_g.reshape(1, w))


def _out_proj_kernel(x_ref, a_ref, s_ref, wa_ref, ws_ref, mod_ref, o_ref):
    f = jnp.dot(a_ref[...], wa_ref[...], preferred_element_type=F32)
    f += jnp.dot(s_ref[...], ws_ref[...], preferred_element_type=F32)
    o_ref[...] = x_ref[...] + mod_ref[2:3, :] * f


def _out_proj(x, attn, ssm, w_attn, w_ssm, mod, rows_per_mod):
    t, d = x.shape
    ka, ks = attn.shape[1], ssm.shape[1]
    tm = _tile(rows_per_mod, 1024)
    tn = _tile(d, 1024)
    tiles_per_mod = rows_per_mod // tm
    return pl.pallas_call(
        _out_proj_kernel,
        out_shape=jax.ShapeDtypeStruct((t, d), F32),
        grid=(t // tm, d // tn),
        in_specs=[
            pl.BlockSpec((tm, tn), lambda i, j: (i, j)),
            pl.BlockSpec((tm, ka), lambda i, j: (i, 0)),
            pl.BlockSpec((tm, ks), lambda i, j: (i, 0)),
            pl.BlockSpec((ka, tn), lambda i, j: (0, j)),
            pl.BlockSpec((ks, tn), lambda i, j: (0, j)),
            pl.BlockSpec((None, MOD_ROWS, tn), lambda i, j: (i // tiles_per_mod, 0, j)),
        ],
        out_specs=pl.BlockSpec((tm, tn), lambda i, j: (i, j)),
        compiler_params=_params("arbitrary", "arbitrary"),
        name="out_proj",
    )(x, attn, ssm, w_attn, w_ssm, mod)


def _swiglu_tile(h, w1, w3, w2):
    a = jnp.dot(h, w1, preferred_element_type=F32)
    b = jnp.dot(h, w3, preferred_element_type=F32)
    return jnp.dot((a * _sigmoid(a) * b).astype(BF16), w2, preferred_element_type=F32)


def _ffn_kernel(x_ref, g_ref, mod_ref, w1_ref, w3_ref, w2_ref, o_ref, h_scr, acc_scr):
    f = pl.program_id(1)

    @pl.when(f == 0)
    def _():
        h_scr[...] = _norm_mod(x_ref[...], g_ref[...], mod_ref[...], 3, 4).astype(BF16)
        acc_scr[...] = jnp.zeros_like(acc_scr)

    acc_scr[...] += _swiglu_tile(h_scr[...], w1_ref[...], w3_ref[...], w2_ref[...])

    @pl.when(f == pl.num_programs(1) - 1)
    def _():
        o_ref[...] = x_ref[...] + mod_ref[5:6, :] * acc_scr[...]


def _ffn(x, g, mod, w1, w3, w2, rows_per_mod):
    t, d = x.shape
    dff = w1.shape[1]
    tm = _tile(rows_per_mod, 512)
    tf = _tile(dff, 512)
    tiles_per_mod = rows_per_mod // tm
    return pl.pallas_call(
        _ffn_kernel,
        out_shape=jax.ShapeDtypeStruct((t, d), F32),
        grid=(t // tm, dff // tf),
        in_specs=[
            pl.BlockSpec((tm, d), lambda i, f: (i, 0)),
            pl.BlockSpec((1, d), lambda i, f: (0, 0)),
            pl.BlockSpec((None, MOD_ROWS, d), lambda i, f: (i // tiles_per_mod, 0, 0)),
            pl.BlockSpec((d, tf), lambda i, f: (0, f)),
            pl.BlockSpec((d, tf), lambda i, f: (0, f)),
            pl.BlockSpec((tf, d), lambda i, f: (f, 0)),
        ],
        out_specs=pl.BlockSpec((tm, d), lambda i, f: (i, 0)),
        scratch_shapes=[pltpu.VMEM((tm, d), BF16), pltpu.VMEM((tm, d), F32)],
        compiler_params=_params("arbitrary", "arbitrary"),
        name="ffn_dense",
    )(x, g.reshape(1, d), mod, w1, w3, w2)


def _router_kernel(x_ref, g_ref, mod_ref, wr_ref, h_ref, r_ref, *, n_experts):
    h = _norm_mod(x_ref[...], g_ref[...], mod_ref[...], 3, 4)
    h_ref[...] = h
    logits = jnp.dot(h.astype(BF16), wr_ref[...], preferred_element_type=F32)
    lane = lax.broadcasted_iota(jnp.int32, logits.shape, 1)
    neg = jnp.float32(-jnp.inf)
    logits = jnp.where(lane < n_experts, logits, neg)
    m1 = jnp.max(logits, axis=-1, keepdims=True)
    i1 = jnp.min(jnp.where(logits == m1, lane, LANES), axis=-1, keepdims=True)
    rest = jnp.where(lane == i1, neg, logits)
    m2 = jnp.max(rest, axis=-1, keepdims=True)
    i2 = jnp.min(jnp.where(rest == m2, lane, LANES), axis=-1, keepdims=True)
    e2 = jnp.exp(m2 - m1)
    den = 1.0 + e2
    r = jnp.where(lane == 0, i1.astype(F32), 0.0)
    r = jnp.where(lane == 1, i2.astype(F32), r)
    r = jnp.where(lane == 2, 1.0 / den, r)
    r = jnp.where(lane == 3, e2 / den, r)
    r_ref[...] = r


def _router(x, g, mod, w_router, rows_per_mod):
    t, d = x.shape
    n_experts = w_router.shape[1]
    wr = jnp.zeros((d, LANES), BF16).at[:, :n_experts].set(w_router.astype(BF16))
    tm = _tile(rows_per_mod, 512)
    tiles_per_mod = rows_per_mod // tm
    return pl.pallas_call(
        functools.partial(_router_kernel, n_experts=n_experts),
        out_shape=(jax.ShapeDtypeStruct((t, d), F32), jax.ShapeDtypeStruct((t, LANES), F32)),
        grid=(t // tm,),
        in_specs=[
            pl.BlockSpec((tm, d), lambda i: (i, 0)),
            pl.BlockSpec((1, d), lambda i: (0, 0)),
            pl.BlockSpec((None, MOD_ROWS, d), lambda i: (i // tiles_per_mod, 0, 0)),
            pl.BlockSpec((d, LANES), lambda i: (0, 0)),
        ],
        out_specs=(pl.BlockSpec((tm, d), lambda i: (i, 0)), pl.BlockSpec((tm, LANES), lambda i: (i, 0))),
        compiler_params=_params("arbitrary"),
        name="moe_router",
    )(x, g.reshape(1, d), mod, wr)


def _gather_rows(idx_ref, src_hbm, dst_ref, sem, n_rows):
    def row_copy(r):
        return pltpu.make_async_copy(src_hbm.at[pl.ds(idx_ref[r], 1), :], dst_ref.at[pl.ds(r, 1), :], sem)

    def start(r, c):
        row_copy(r).start()
        return c

    def wait(r, c):
        row_copy(r).wait()
        return c

    lax.fori_loop(0, n_rows, start, 0)
    lax.fori_loop(0, n_rows, wait, 0)


def _moe_kernel(be_ref, nu_ref, tok_ref, h_hbm, gate_ref, w1_ref, w3_ref, w2_ref, o_ref,
                xs_scr, hb_scr, acc_scr, sem, *, rows):
    i = pl.program_id(0)
    f = pl.program_id(1)
    used = i < nu_ref[0]

    @pl.when(jnp.logical_and(used, f == 0))
    def _():
        _gather_rows(tok_ref.at[0, 0], h_hbm, xs_scr, sem, rows)
        hb_scr[...] = xs_scr[...].astype(BF16)
        acc_scr[...] = jnp.zeros_like(acc_scr)

    @pl.when(used)
    def _():
        acc_scr[...] += _swiglu_tile(hb_scr[...], w1_ref[...], w3_ref[...], w2_ref[...])

    @pl.when(f == pl.num_programs(1) - 1)
    def _():
        @pl.when(used)
        def _():
            o_ref[...] = acc_scr[...] * gate_ref[...]

        @pl.when(jnp.logical_not(used))
        def _():
            o_ref[...] = jnp.zeros_like(o_ref)


def _moe_ffn(h, slot_tok, slot_gate, block_e, n_used, w1, w3, w2):
    d = h.shape[1]
    dff = w1.shape[2]
    n_slots = slot_tok.shape[0]
    rows = MOE_ROWS
    nb = n_slots // rows
    tf = _tile(dff, 512)
    nf = dff // tf

    def wcol(i, f, be, nu):
        return (be[i], 0, jnp.where(i < nu[0], f, nf - 1))

    def wrow(i, f, be, nu):
        return (be[i], jnp.where(i < nu[0], f, nf - 1), 0)

    return pl.pallas_call(
        functools.partial(_moe_kernel, rows=rows),
        out_shape=jax.ShapeDtypeStruct((n_slots, d), F32),
        grid_spec=pltpu.PrefetchScalarGridSpec(
            num_scalar_prefetch=2,
            grid=(nb, nf),
            in_specs=[
                pl.BlockSpec((1, 1, rows), lambda i, f, be, nu: (i, 0, 0), memory_space=pltpu.SMEM),
                pl.BlockSpec(memory_space=pl.ANY),
                pl.BlockSpec((rows, 1), lambda i, f, be, nu: (i, 0)),
                pl.BlockSpec((None, d, tf), wcol),
                pl.BlockSpec((None, d, tf), wcol),
                pl.BlockSpec((None, tf, d), wrow),
            ],
            out_specs=pl.BlockSpec((rows, d), lambda i, f, be, nu: (i, 0)),
            scratch_shapes=[pltpu.VMEM((rows, d), F32), pltpu.VMEM((rows, d), BF16),
                            pltpu.VMEM((rows, d), F32), pltpu.SemaphoreType.DMA],
        ),
        compiler_params=_params("arbitrary", "arbitrary"),
        name="moe_ffn",
    )(block_e, n_used, slot_tok.reshape(nb, 1, rows), h, slot_gate.reshape(n_slots, 1), w1, w3, w2)


def _moe_combine_kernel(s0_ref, s1_ref, x_ref, mod_ref, ys_hbm, o_ref, y0_scr, y1_scr, sem, *, rows):
    _gather_rows(s0_ref.at[0, 0], ys_hbm, y0_scr, sem, rows)
    _gather_rows(s1_ref.at[0, 0], ys_hbm, y1_scr, sem, rows)
    o_ref[...] = x_ref[...] + mod_ref[5:6, :] * (y0_scr[...] + y1_scr[...])


def _moe_combine(x, mod, ys, slot0, slot1, rows_per_mod):
    t, d = x.shape
    tm = _tile(rows_per_mod, 256)
    nt = t // tm
    tiles_per_mod = rows_per_mod // tm
    idx_spec = pl.BlockSpec((1, 1, tm), lambda i: (i, 0, 0), memory_space=pltpu.SMEM)
    return pl.pallas_call(
        functools.partial(_moe_combine_kernel, rows=tm),
        out_shape=jax.ShapeDtypeStruct((t, d), F32),
        grid=(nt,),
        in_specs=[
            idx_spec, idx_spec,
            pl.BlockSpec((tm, d), lambda i: (i, 0)),
            pl.BlockSpec((None, MOD_ROWS, d), lambda i: (i // tiles_per_mod, 0, 0)),
            pl.BlockSpec(memory_space=pl.ANY),
        ],
        out_specs=pl.BlockSpec((tm, d), lambda i: (i, 0)),
        scratch_shapes=[pltpu.VMEM((tm, d), F32), pltpu.VMEM((tm, d), F32), pltpu.SemaphoreType.DMA],
        compiler_params=_params("arbitrary"),
        name="moe_combine",
    )(slot0.reshape(nt, 1, tm), slot1.reshape(nt, 1, tm), x, mod, ys)


def _moe(x, g, mod, w_router, w1, w3, w2, rows_per_mod):
    t, d = x.shape
    n_experts = w_router.shape[1]
    h, r = _router(x, g, mod, w_router, rows_per_mod)
    e_flat = r[:, :TOP_K].astype(jnp.int32).reshape(-1)
    gate_flat = r[:, TOP_K:2 * TOP_K].reshape(-1)
    n_assign = t * TOP_K
    onehot = (e_flat[:, None] == jnp.arange(n_experts, dtype=jnp.int32)[None, :]).astype(jnp.int32)
    csum = jnp.cumsum(onehot, axis=0)
    rank = jnp.take_along_axis(csum, e_flat[:, None], axis=1)[:, 0] - 1
    counts = csum[-1]
    padded = (counts + MOE_ROWS - 1) // MOE_ROWS * MOE_ROWS
    pad_end = jnp.cumsum(padded)
    pad_start = pad_end - padded
    slot = (pad_start[e_flat] + rank).astype(jnp.int32)
    nb = -(-n_assign // MOE_ROWS) + n_experts
    n_slots = nb * MOE_ROWS
    tok = jnp.arange(n_assign, dtype=jnp.int32) // TOP_K
    slot_tok = jnp.zeros((n_slots,), jnp.int32).at[slot].set(tok)
    slot_gate = jnp.zeros((n_slots,), F32).at[slot].set(gate_flat)
    block_e = jnp.minimum(jnp.searchsorted(pad_end, jnp.arange(nb) * MOE_ROWS, side='right'),
                          n_experts - 1).astype(jnp.int32)
    n_used = (pad_end[-1:] // MOE_ROWS).astype(jnp.int32)
    ys = _moe_ffn(h, slot_tok, slot_gate, block_e, n_used, w1, w3, w2)
    slot2 = slot.reshape(t, TOP_K)
    return _moe_combine(x, mod, ys, slot2[:, 0], slot2[:, 1], rows_per_mod)


def _final_norm_kernel(x_ref, g_ref, o_ref):
    x = x_ref[...]
    o_ref[...] = x * lax.rsqrt(jnp.mean(x * x, axis=-1, keepdims=True) + EPS) * g_ref[...]


def _final_norm(x, g):
    t, d = x.shape
    tm = _tile(t, 1024)
    return pl.pallas_call(
        _final_norm_kernel,
        out_shape=jax.ShapeDtypeStruct((t, d), F32),
        grid=(t // tm,),
        in_specs=[pl.BlockSpec((tm, d), lambda i: (i, 0)), pl.BlockSpec((1, d), lambda i: (0, 0))],
        out_specs=pl.BlockSpec((tm, d), lambda i: (i, 0)),
        compiler_params=_params("arbitrary"),
        name="final_norm",
    )(x, g.reshape(1, d))


def kernel(x_prompt, x_sample, cache_k, cache_v, state_ssm_re, state_ssm_im, c, c_ctx, w_mod, b_mod, norm1_g, norm2_g, w_in, w_out, lam_q1, lam_k1, lam_q2, lam_k2, head_g, ssm_a_re, ssm_a_im, ssm_log_dt, ssm_b_re, ssm_b_im, ssm_c_re, ssm_c_im, ssm_d, w_glu, ssm_norm_g, ffn_w1, ffn_w3, ffn_w2, moe_router, moe_w1, moe_w3, moe_w2, final_g):
    batch, seq, d = x_prompt.shape
    dec_batch, dec_seq, _ = x_sample.shape
    depth = w_mod.shape[0]
    _, _, n_past, n_heads, _, qk_dim = cache_k.shape
    v_dim = cache_v.shape[-1]
    n_groups, n_state = state_ssm_re.shape[-2:]
    ssm_w = w_glu.shape[-1]
    qk_w = n_heads * 2 * qk_dim
    attn_w = n_heads * v_dim
    u_off = 2 * qk_w + attn_w
    assert u_off % ssm_w == 0 and ssm_w % LANES == 0

    n_cond = 1 + dec_batch
    cond = jnp.zeros((-(-n_cond // 8) * 8, d), F32).at[0].set(c_ctx).at[1:n_cond].set(c)
    mods = _adaln(cond, w_mod, b_mod)[:, :n_cond].reshape(depth, n_cond, N_MOD, d)
    mods = jnp.pad(mods, ((0, 0), (0, 0), (0, MOD_ROWS - N_MOD), (0, 0)))

    cos, sin = _rope_tables(dec_seq, qk_dim)
    ck = cache_k.reshape(dec_batch, depth, n_past, qk_w)
    cv = cache_v.reshape(dec_batch, depth, n_past, attn_w)

    def to_groups(s):
        return jnp.transpose(s, (2, 0, 1, 3)).reshape(n_groups, s.shape[0], 2 * n_state)

    xp = x_prompt.reshape(batch * seq, d)
    xs = x_sample.reshape(dec_batch * dec_seq, d)
    zeros_state = jnp.zeros((n_groups, batch, 2 * n_state), F32)
    new_k, new_v, new_re, new_im = [], [], [], []
    for l in range(depth):
        w_in_l = w_in[l].astype(BF16)
        w_out_a = w_out[l, :attn_w].astype(BF16)
        w_out_s = w_out[l, attn_w:].astype(BF16)
        w_glu_l = w_glu[l].astype(BF16)
        lam = (jnp.exp(jnp.sum(lam_q1[l] * lam_k1[l])) - jnp.exp(jnp.sum(lam_q2[l] * lam_k2[l]))
               + _lambda_init(l)).reshape(1).astype(F32)
        mats = _ssm_matrices(ssm_a_re[l], ssm_a_im[l], ssm_log_dt[l], ssm_b_re[l], ssm_b_im[l],
                             ssm_c_re[l], ssm_c_im[l])
        if l % 2 == 0:
            ffn_w = (ffn_w1[l // 2].astype(BF16), ffn_w3[l // 2].astype(BF16), ffn_w2[l // 2].astype(BF16))
        else:
            ffn_w = (moe_w1[l // 2].astype(BF16), moe_w3[l // 2].astype(BF16), moe_w2[l // 2].astype(BF16))

        def layer(x, mod, b, s, rows_per_mod, past, tabs, h0re, h0im):
            z = _in_proj(x, norm1_g[l], mod, w_in_l, rows_per_mod)
            attn = _attention(z, lam, head_g[l], l, b, s, n_heads, qk_dim, v_dim, past, tabs)
            y, fre, fim = _ssm(z[:, u_off:], mats, h0re, h0im, b, s)
            ssm = _ssm_post(y, z, u_off // ssm_w, ssm_d[l], w_glu_l, ssm_norm_g[l])
            x = _out_proj(x, attn, ssm, w_out_a, w_out_s, mod, rows_per_mod)
            if l % 2 == 0:
                x = _ffn(x, norm2_g[l], mod, *ffn_w, rows_per_mod)
            else:
                x = _moe(x, norm2_g[l], mod, moe_router[l // 2], *ffn_w, rows_per_mod)
            return x, z, fre, fim

        xp, zp, fre, fim = layer(xp, mods[l, :1], batch, seq, batch * seq, None, None, zeros_state, zeros_state)
        new_k.append(zp[:, qk_w:2 * qk_w].reshape(batch, seq, n_heads, 2, qk_dim))
        new_v.append(zp[:, 2 * qk_w:u_off].reshape(batch, seq, n_heads, v_dim))
        new_re.append(jnp.transpose(fre.reshape(n_groups, batch, 2, n_state), (1, 2, 0, 3)))
        new_im.append(jnp.transpose(fim.reshape(n_groups, batch, 2, n_state), (1, 2, 0, 3)))
        xs, _, _, _ = layer(xs, mods[l, 1:], dec_batch, dec_seq, dec_seq, (ck, cv), (cos, sin),
                            to_groups(state_ssm_re[:, l]), to_groups(state_ssm_im[:, l]))

    y_prompt = _final_norm(xp, final_g).reshape(batch, seq, d)
    y_sample = _final_norm(xs, final_g).reshape(dec_batch, dec_seq, d)
    return (y_prompt, y_sample, jnp.stack(new_k, axis=1), jnp.stack(new_v, axis=1),
            jnp.stack(new_re, axis=1), jnp.stack(new_im, axis=1))
```

```python
import functools
import math

import jax
import jax.numpy as jnp
from jax import lax
from jax.experimental import pallas as pl
from jax.experimental.pallas import tpu as pltpu

F32 = jnp.float32
BF16 = jnp.bfloat16

EPS = 1e-6
GRID_W = 64
ROPE_BASE = 10000.0
TOP_K = 2
SSM_CHUNK = 16
MOE_ROWS = 512
N_MOD = 6
MOD_ROWS = 8
LANES = 128
FFN_PIECE = 256
SCORE_LEAD = 5
VMEM_LIMIT_BYTES = 56 * 1024 * 1024


def _lambda_init(l):
    return 0.8 - 0.6 * math.exp(-0.3 * l)


def _tile(dim, pref):
    if dim <= pref:
        return dim
    t = pref
    while dim % t:
        t -= 8
    assert t > 0
    return t


def _params(*sem):
    return pltpu.CompilerParams(dimension_semantics=sem, vmem_limit_bytes=VMEM_LIMIT_BYTES)


def _sigmoid(x):
    return 1.0 / (1.0 + jnp.exp(-x))


def _norm_mod(x, g, mod, shift_row, scale_row):
    y = x * lax.rsqrt(jnp.mean(x * x, axis=-1, keepdims=True) + EPS) * g
    return y * (1.0 + mod[scale_row:scale_row + 1, :]) + mod[shift_row:shift_row + 1, :]


def _adaln_kernel(c_ref, w_ref, b_ref, o_ref):
    c = c_ref[...]
    s = (c * _sigmoid(c)).astype(BF16)
    o_ref[...] = jnp.dot(s, w_ref[...].astype(BF16), preferred_element_type=F32) + b_ref[...]


def _adaln(cond, w_mod, b_mod):
    depth, d, n = w_mod.shape
    r = cond.shape[0]
    tn = _tile(n, 1024)
    return pl.pallas_call(
        _adaln_kernel,
        out_shape=jax.ShapeDtypeStruct((depth, r, n), F32),
        grid=(depth, n // tn),
        in_specs=[
            pl.BlockSpec((r, d), lambda l, j: (0, 0)),
            pl.BlockSpec((None, d, tn), lambda l, j: (l, 0, j)),
            pl.BlockSpec((None, 1, tn), lambda l, j: (l, 0, j)),
        ],
        out_specs=pl.BlockSpec((None, r, tn), lambda l, j: (l, 0, j)),
        compiler_params=_params("arbitrary", "arbitrary"),
        name="adaln",
    )(cond, w_mod, b_mod.reshape(depth, 1, n))


def _in_proj_kernel(x_ref, g_ref, mod_ref, w_ref, o_ref, h_scr):
    @pl.when(pl.program_id(1) == 0)
    def _():
        h_scr[...] = _norm_mod(x_ref[...], g_ref[...], mod_ref[...], 0, 1).astype(BF16)

    o_ref[...] = jnp.dot(h_scr[...], w_ref[...], preferred_element_type=F32)


def _in_proj(x, g, mod, w, rows_per_mod):
    t, d = x.shape
    n = w.shape[1]
    tm = _tile(rows_per_mod, 1024)
    tn = _tile(n, 1024)
    tiles_per_mod = rows_per_mod // tm
    return pl.pallas_call(
        _in_proj_kernel,
        out_shape=jax.ShapeDtypeStruct((t, n), F32),
        grid=(t // tm, n // tn),
        in_specs=[
            pl.BlockSpec((tm, d), lambda i, j: (i, 0)),
            pl.BlockSpec((1, d), lambda i, j: (0, 0)),
            pl.BlockSpec((None, MOD_ROWS, d), lambda i, j: (i // tiles_per_mod, 0, 0)),
            pl.BlockSpec((d, tn), lambda i, j: (0, j)),
        ],
        out_specs=pl.BlockSpec((tm, tn), lambda i, j: (i, j)),
        scratch_shapes=[pltpu.VMEM((tm, d), BF16)],
        compiler_params=_params("arbitrary", "arbitrary"),
        name="in_proj",
    )(x, g.reshape(1, d), mod, w)


def _rope(x, cos, sin_signed):
    lane = lax.broadcasted_iota(jnp.int32, x.shape, 1)
    first_half = (lane % 32) < 16
    partner = jnp.where(first_half, pltpu.roll(x, LANES - 16, 1), pltpu.roll(x, 16, 1))
    return x * cos + partner * sin_signed


def _attn_kernel(*refs, n_past, rope, qk_dim, out_scale, key_chunk):
    if rope:
        (lam_ref, q_ref, k_ref, v_ref, ck_ref, cv_ref, cq_ref, sq_ref, ckk_ref, skk_ref, hg_ref,
         o_ref, k0_scr, k1_scr, vt_scr) = refs
    else:
        lam_ref, q_ref, k_ref, v_ref, hg_ref, o_ref, k0_scr, k1_scr, vt_scr = refs

    @pl.when(pl.program_id(2) == 0)
    def _():
        k = k_ref[...]
        if rope:
            k = _rope(k, ckk_ref[...], skk_ref[...])
        kb = k.astype(BF16)
        if n_past:
            ck = ck_ref[...].astype(BF16)
            k0_scr[0:n_past, :] = ck[:, :qk_dim]
            k1_scr[0:n_past, :] = ck[:, qk_dim:]
            vt_scr[:, 0:n_past] = cv_ref[...].T.astype(BF16)
        k0_scr[n_past:, :] = kb[:, :qk_dim]
        k1_scr[n_past:, :] = kb[:, qk_dim:]
        vt_scr[:, n_past:] = v_ref[...].T.astype(BF16)

    q = q_ref[...]
    if rope:
        q = _rope(q, cq_ref[...], sq_ref[...])
    qb = (q * (qk_dim ** -0.5 * math.log2(math.e))).astype(BF16)
    nt = (((1,), (1,)), ((), ()))

    tq = q_ref.shape[0]
    nk, v_dim = vt_scr.shape[1], vt_scr.shape[0]
    qh = (qb[:, :qk_dim], qb[:, qk_dim:])
    k_scr = (k0_scr, k1_scr)
    m = [jnp.full((1, tq), -jnp.inf, F32)] * 2
    l = [jnp.zeros((1, tq), F32)] * 2
    acc = [jnp.zeros((v_dim, tq), F32)] * 2
    n_items = 2 * (nk // key_chunk)
    scores = {}
    for t in range(n_items + SCORE_LEAD):
        if t < n_items:
            c, a = divmod(t, 2)
            scores[t] = lax.dot_general(k_scr[a][c * key_chunk:(c + 1) * key_chunk, :], qh[a], nt,
                                        preferred_element_type=F32)
        if t >= SCORE_LEAD:
            c, a = divmod(t - SCORE_LEAD, 2)
            s = scores.pop(t - SCORE_LEAD)
            m_new = jnp.maximum(m[a], jnp.max(s, axis=0, keepdims=True))
            alpha = jnp.exp2(m[a] - m_new)
            p = jnp.exp2(s - m_new)
            l[a] = alpha * l[a] + jnp.sum(p, axis=0, keepdims=True)
            acc[a] = alpha * acc[a] + jnp.dot(vt_scr[:, c * key_chunk:(c + 1) * key_chunk], p.astype(BF16),
                                              preferred_element_type=F32)
            m[a] = m_new
    o = acc[0] * (1.0 / l[0]) - acc[1] * (lam_ref[0] / l[1])
    o = o * lax.rsqrt(jnp.mean(o * o, axis=0, keepdims=True) + EPS) * (hg_ref[...] * out_scale)
    o_ref[...] = o.T.astype(o_ref.dtype)


def _attention(z, lam, head_g, layer, batch, seq, n_heads, qk_dim, v_dim, past=None, rope_tabs=None):
    assert 2 * qk_dim == LANES and v_dim == LANES
    t = batch * seq
    tq = _tile(seq, 256)
    nq = seq // tq
    rope = past is not None
    n_past = past[0].shape[2] if rope else 0
    nk = n_past + seq
    q_spec = pl.BlockSpec((tq, LANES), lambda b, h, i: (b * nq + i, h))
    k_spec = pl.BlockSpec((seq, LANES), lambda b, h, i: (b, n_heads + h))
    v_spec = pl.BlockSpec((seq, LANES), lambda b, h, i: (b, 2 * n_heads + h))
    smem = pl.BlockSpec(memory_space=pltpu.SMEM)
    hg_spec = pl.BlockSpec((LANES, 1), lambda b, h, i: (0, 0))
    if rope:
        ck, cv = past
        cos, sin = rope_tabs
        in_specs = [
            smem, q_spec, k_spec, v_spec,
            pl.BlockSpec((None, None, n_past, LANES), lambda b, h, i: (b, layer, 0, h)),
            pl.BlockSpec((None, None, n_past, LANES), lambda b, h, i: (b, layer, 0, h)),
            pl.BlockSpec((tq, LANES), lambda b, h, i: (i, 0)),
            pl.BlockSpec((tq, LANES), lambda b, h, i: (i, 0)),
            pl.BlockSpec((seq, LANES), lambda b, h, i: (0, 0)),
            pl.BlockSpec((seq, LANES), lambda b, h, i: (0, 0)),
            hg_spec,
        ]
        args = (lam, z, z, z, ck, cv, cos, sin, cos, sin, head_g.reshape(LANES, 1))
    else:
        in_specs = [smem, q_spec, k_spec, v_spec, hg_spec]
        args = (lam, z, z, z, head_g.reshape(LANES, 1))
    return pl.pallas_call(
        functools.partial(_attn_kernel, n_past=n_past, rope=rope, qk_dim=qk_dim,
                          out_scale=1.0 - _lambda_init(layer), key_chunk=_tile(nk, 256)),
        out_shape=jax.ShapeDtypeStruct((t, n_heads * v_dim), BF16),
        grid=(batch, n_heads, nq),
        in_specs=in_specs,
        out_specs=pl.BlockSpec((tq, LANES), lambda b, h, i: (b * nq + i, h)),
        scratch_shapes=[pltpu.VMEM((nk, qk_dim), BF16), pltpu.VMEM((nk, qk_dim), BF16),
                        pltpu.VMEM((v_dim, nk), BF16)],
        compiler_params=_params("arbitrary", "arbitrary", "arbitrary"),
        name="diff_attn",
    )(*args)


def _rope_tables(seq, qk_dim):
    half = qk_dim // 2
    inv = 1.0 / (ROPE_BASE ** (jnp.arange(0, half, 2, dtype=F32) / half))
    pos = jnp.arange(seq)
    sign = jnp.concatenate([-jnp.ones((half // 2,), F32), jnp.ones((half // 2,), F32)])

    def tab(p):
        ang = p.astype(F32)[:, None] * inv[None, :]
        cos = jnp.concatenate([jnp.cos(ang), jnp.cos(ang)], -1)
        sin = jnp.concatenate([jnp.sin(ang), jnp.sin(ang)], -1) * sign[None, :]
        return cos, sin

    cr, sr = tab(pos // GRID_W)
    cc, sc = tab(pos % GRID_W)
    cos = jnp.concatenate([cr, cc], -1)
    sin = jnp.concatenate([sr, sc], -1)
    reps = LANES // qk_dim
    return jnp.tile(cos, (1, reps)), jnp.tile(sin, (1, reps))


def _ssm_kernel(x_ref, m_ref, bre_ref, bim_ref, cre_ref, cim_ref, are_ref, aim_ref, h0re_ref, h0im_ref,
                y_ref, fre_ref, fim_ref, sre_scr, sim_scr, hfre_scr, hfim_scr, hbre_scr, hbim_scr,
                *, n_chunks, batch, state):
    xb = x_ref[...].astype(BF16)
    sre_scr[...] = jnp.dot(xb, bre_ref[...], preferred_element_type=F32)
    sim_scr[...] = jnp.dot(xb, bim_ref[...], preferred_element_type=F32)
    ar = are_ref[...]
    ai = aim_ref[...]
    fwd_lane = lax.broadcasted_iota(jnp.int32, (batch, 2 * state), 1) < state

    def step(i, carry):
        hre, him = carry
        rf = pl.ds(pl.multiple_of(i * batch, batch), batch)
        rb = pl.ds(pl.multiple_of((n_chunks - 1 - i) * batch, batch), batch)
        hfre_scr[rf, :] = hre
        hfim_scr[rf, :] = him
        hbre_scr[rb, :] = hre
        hbim_scr[rb, :] = him
        sre = jnp.where(fwd_lane, sre_scr[rf, :], sre_scr[rb, :])
        sim = jnp.where(fwd_lane, sim_scr[rf, :], sim_scr[rb, :])
        return ar * hre - ai * him + sre, ar * him + ai * hre + sim

    hre, him = lax.fori_loop(0, n_chunks, step, (h0re_ref[...], h0im_ref[...]))
    fre_ref[...] = hre
    fim_ref[...] = him
    lane = lax.broadcasted_iota(jnp.int32, hfre_scr.shape, 1) < state
    hs_re = jnp.where(lane, hfre_scr[...], hbre_scr[...]).astype(BF16)
    hs_im = jnp.where(lane, hfim_scr[...], hbim_scr[...]).astype(BF16)
    y = jnp.dot(xb, m_ref[...], preferred_element_type=F32)
    y += jnp.dot(hs_re, cre_ref[...], preferred_element_type=F32)
    y += jnp.dot(hs_im, cim_ref[...], preferred_element_type=F32)
    y_ref[...] = y


def _ssm_matrices(a_re, a_im, log_dt, b_re, b_im, c_re, c_im):
    tc = SSM_CHUNK
    hi = lax.Precision.HIGHEST
    g, p = a_re.shape[1:]
    ch = b_re.shape[-1]
    tau = jnp.arange(tc + 1, dtype=F32)
    s_idx = jnp.arange(tc)
    m_tot = 0.0
    b_parts, c_parts, a_parts = [], [], []
    for d in range(2):
        lam = lax.complex(a_re[d].astype(F32), a_im[d].astype(F32))
        lam_dt = lam * jnp.exp(log_dt[d].astype(F32))[:, None]
        a_bar = jnp.exp(lam_dt)
        b_bar = ((a_bar - 1.0) / lam)[..., None] * lax.complex(b_re[d].astype(F32), b_im[d].astype(F32))
        c_mat = lax.complex(c_re[d].astype(F32), c_im[d].astype(F32))
        pw = jnp.exp(tau[:, None, None] * lam_dt[None])
        kern = jnp.einsum('gcp,tgp,gpd->tgcd', c_mat, pw[:tc], b_bar, precision=hi).real
        lag = (s_idx[None, :] - s_idx[:, None]) if d == 0 else (s_idx[:, None] - s_idx[None, :])
        valid = (lag >= 0)
        kk = kern[jnp.clip(lag, 0, tc - 1)]
        kk = jnp.where(valid[:, :, None, None, None], kk, 0.0)
        m_tot = m_tot + jnp.transpose(kk, (2, 0, 4, 1, 3)).reshape(g, tc * ch, tc * ch)
        pw_in = pw[tc - 1 - s_idx] if d == 0 else pw[s_idx]
        bs = pw_in[:, :, :, None] * b_bar[None]
        b_parts.append(jnp.transpose(bs, (1, 0, 3, 2)).reshape(g, tc * ch, p))
        pw_out = pw[s_idx + 1] if d == 0 else pw[tc - s_idx]
        cs = c_mat[None] * pw_out[:, :, None, :]
        c_parts.append(jnp.transpose(cs, (1, 3, 0, 2)).reshape(g, p, tc * ch))
        a_parts.append(pw[tc])
    bcat = jnp.concatenate(b_parts, axis=-1)
    ccat = jnp.concatenate(c_parts, axis=1)
    acat = jnp.concatenate(a_parts, axis=-1)[:, None, :]
    return (m_tot.astype(BF16), bcat.real.astype(BF16), bcat.imag.astype(BF16),
            ccat.real.astype(BF16), (-ccat.imag).astype(BF16), acat.real, acat.imag)


def _chunk_rows(seq):
    nc = seq // SSM_CHUNK
    rb = _tile(nc, 128)
    return nc, rb, nc // rb


def _ssm_pack_kernel(*refs, rb, ch):
    u_refs, x_ref = refs[:-1], refs[-1]
    per_tile = LANES // ch
    slot = lax.broadcasted_iota(jnp.int32, (rb, LANES), 1) // ch
    for j, u_ref in enumerate(u_refs):
        for half in range(SSM_CHUNK // per_tile):
            acc = [None] * per_tile
            for tt in range(per_tile):
                v = u_ref[pl.ds(half * per_tile + tt, rb, stride=SSM_CHUNK), :]
                for gg in range(per_tile):
                    shift = ((tt - gg) % per_tile) * ch
                    r = pltpu.roll(v, shift, 1) if shift else v
                    acc[gg] = r if tt == 0 else jnp.where(slot == tt, r, acc[gg])
            for gg in range(per_tile):
                x_ref[j * per_tile + gg, :, half * LANES:(half + 1) * LANES] = acc[gg]


def _ssm_pack(z, u_col_block, width, n_groups, batch, seq):
    ch = width // n_groups
    kc = SSM_CHUNK * ch
    assert LANES % ch == 0 and SSM_CHUNK % (LANES // ch) == 0 and kc % LANES == 0
    nc, rb, nbk = _chunk_rows(seq)
    n_tiles = width // LANES
    first_tile = u_col_block * n_tiles
    x = pl.pallas_call(
        functools.partial(_ssm_pack_kernel, rb=rb, ch=ch),
        out_shape=jax.ShapeDtypeStruct((n_groups, nc, batch * kc), F32),
        grid=(batch * nbk,),
        in_specs=[pl.BlockSpec((rb * SSM_CHUNK, LANES), functools.partial(lambda i, col: (i, col), col=first_tile + j))
                  for j in range(n_tiles)],
        out_specs=pl.BlockSpec((n_groups, rb, kc), lambda i: (0, i % nbk, i // nbk)),
        compiler_params=_params("arbitrary"),
        name="s5_pack",
    )(*([z] * n_tiles))
    return x.reshape(n_groups, nc * batch, kc)


def _ssm(x, mats, h0re, h0im, batch, seq):
    m, bre, bim, cre, cim, are, aim = mats
    g, kc, _ = m.shape
    p2 = bre.shape[-1]
    nc = seq // SSM_CHUNK
    n = nc * batch
    grp = lambda shape: pl.BlockSpec((None,) + shape, lambda i: (i,) + (0,) * len(shape))
    y, fre, fim = pl.pallas_call(
        functools.partial(_ssm_kernel, n_chunks=nc, batch=batch, state=p2 // 2),
        out_shape=(jax.ShapeDtypeStruct((g, n, kc), F32),
                   jax.ShapeDtypeStruct((g, batch, p2), F32),
                   jax.ShapeDtypeStruct((g, batch, p2), F32)),
        grid=(g,),
        in_specs=[grp((n, kc)), grp((kc, kc)), grp((kc, p2)), grp((kc, p2)), grp((p2, kc)), grp((p2, kc)),
                  grp((1, p2)), grp((1, p2)), grp((batch, p2)), grp((batch, p2))],
        out_specs=(grp((n, kc)), grp((batch, p2)), grp((batch, p2))),
        scratch_shapes=[pltpu.VMEM((n, p2), F32)] * 6,
        compiler_params=_params("arbitrary"),
        name="s5_scan",
    )(x, m, bre, bim, cre, cim, are, aim, h0re, h0im)
    return y, fre, fim


def _ssm_post_kernel(y_ref, u_ref, d_ref, w_ref, g_ref, o_ref, y_scr, *, rb, ch):
    per_tile = LANES // ch
    slot = lax.broadcasted_iota(jnp.int32, (rb, LANES), 1) // ch
    n_tiles = y_scr.shape[0]
    for j in range(n_tiles):
        for half in range(SSM_CHUNK // per_tile):
            for tt in range(per_tile):
                out = None
                for gg in range(per_tile):
                    src = y_ref[j * per_tile + gg, :, half * LANES:(half + 1) * LANES]
                    shift = ((gg - tt) % per_tile) * ch
                    r = pltpu.roll(src, shift, 1) if shift else src
                    out = r if gg == 0 else jnp.where(slot == gg, r, out)
                y_scr[j, pl.ds(half * per_tile + tt, rb, stride=SSM_CHUNK), :] = out
    y = jnp.concatenate([y_scr[j] for j in range(n_tiles)], axis=1) + d_ref[...] * u_ref[...]
    y = 0.5 * y * (1.0 + jnp.tanh(math.sqrt(2.0 / math.pi) * (y + 0.044715 * (y * y * y))))
    y = y * _sigmoid(jnp.dot(y.astype(BF16), w_ref[...], preferred_element_type=F32))
    y = y * lax.rsqrt(jnp.mean(y * y, axis=-1, keepdims=True) + EPS) * g_ref[...]
    o_ref[...] = y.astype(o_ref.dtype)


def _ssm_post(y, z, u_col_block, ssm_d, w_glu, norm_g, batch, seq):
    g, _, kc = y.shape
    w = w_glu.shape[0]
    nc, rb, nbk = _chunk_rows(seq)
    tm = rb * SSM_CHUNK
    fixed = lambda i: (0, 0)
    return pl.pallas_call(
        functools.partial(_ssm_post_kernel, rb=rb, ch=w // g),
        out_shape=jax.ShapeDtypeStruct((batch * seq, w), BF16),
        grid=(batch * nbk,),
        in_specs=[pl.BlockSpec((g, rb, kc), lambda i: (0, i % nbk, i // nbk)),
                  pl.BlockSpec((tm, w), lambda i: (i, u_col_block)),
                  pl.BlockSpec((1, w), fixed), pl.BlockSpec((w, w), fixed), pl.BlockSpec((1, w), fixed)],
        out_specs=pl.BlockSpec((tm, w), lambda i: (i, 0)),
        scratch_shapes=[pltpu.VMEM((w // LANES, tm, LANES), F32)],
        compiler_params=_params("arbitrary"),
        name="s5_post",
    )(y.reshape(g, nc, batch * kc), z, ssm_d.reshape(1, w), w_glu, norm_g.reshape(1, w))


def _out_proj_kernel(x_ref, a_ref, s_ref, wa_ref, ws_ref, mod_ref, o_ref):
    f = jnp.dot(a_ref[...], wa_ref[...], preferred_element_type=F32)
    f += jnp.dot(s_ref[...], ws_ref[...], preferred_element_type=F32)
    o_ref[...] = x_ref[...] + mod_ref[2:3, :] * f


def _out_proj(x, attn, ssm, w_attn, w_ssm, mod, rows_per_mod):
    t, d = x.shape
    ka, ks = attn.shape[1], ssm.shape[1]
    tm = _tile(rows_per_mod, 1024)
    tn = _tile(d, 1024)
    tiles_per_mod = rows_per_mod // tm
    return pl.pallas_call(
        _out_proj_kernel,
        out_shape=jax.ShapeDtypeStruct((t, d), F32),
        grid=(t // tm, d // tn),
        in_specs=[
            pl.BlockSpec((tm, tn), lambda i, j: (i, j)),
            pl.BlockSpec((tm, ka), lambda i, j: (i, 0)),
            pl.BlockSpec((tm, ks), lambda i, j: (i, 0)),
            pl.BlockSpec((ka, tn), lambda i, j: (0, j)),
            pl.BlockSpec((ks, tn), lambda i, j: (0, j)),
            pl.BlockSpec((None, MOD_ROWS, tn), lambda i, j: (i // tiles_per_mod, 0, j)),
        ],
        out_specs=pl.BlockSpec((tm, tn), lambda i, j: (i, j)),
        compiler_params=_params("arbitrary", "arbitrary"),
        name="out_proj",
    )(x, attn, ssm, w_attn, w_ssm, mod)


def _swiglu_tile(h, w1_ref, w3_ref, w2_ref):
    tf = w1_ref.shape[1]
    piece = FFN_PIECE if tf % FFN_PIECE == 0 else tf
    cols = [slice(k, k + piece) for k in range(0, tf, piece)]

    def up(c):
        return (jnp.dot(h, w1_ref[:, c], preferred_element_type=F32),
                jnp.dot(h, w3_ref[:, c], preferred_element_type=F32))

    out = None
    ab = up(cols[0])
    for k, c in enumerate(cols):
        a, b = ab
        if k + 1 < len(cols):
            ab = up(cols[k + 1])
        g = (a * _sigmoid(a) * b).astype(BF16)
        y = jnp.dot(g, w2_ref[c, :], preferred_element_type=F32)
        out = y if out is None else out + y
    return out


def _ffn_kernel(x_ref, g_ref, mod_ref, w1_ref, w3_ref, w2_ref, o_ref, h_scr, acc_scr):
    f = pl.program_id(1)

    @pl.when(f == 0)
    def _():
        h_scr[...] = _norm_mod(x_ref[...], g_ref[...], mod_ref[...], 3, 4).astype(BF16)
        acc_scr[...] = jnp.zeros_like(acc_scr)

    acc_scr[...] += _swiglu_tile(h_scr[...], w1_ref, w3_ref, w2_ref)

    @pl.when(f == pl.num_programs(1) - 1)
    def _():
        o_ref[...] = x_ref[...] + mod_ref[5:6, :] * acc_scr[...]


def _ffn(x, g, mod, w1, w3, w2, rows_per_mod):
    t, d = x.shape
    dff = w1.shape[1]
    tm = _tile(rows_per_mod, 512)
    tf = _tile(dff, 512)
    tiles_per_mod = rows_per_mod // tm
    return pl.pallas_call(
        _ffn_kernel,
        out_shape=jax.ShapeDtypeStruct((t, d), F32),
        grid=(t // tm, dff // tf),
        in_specs=[
            pl.BlockSpec((tm, d), lambda i, f: (i, 0)),
            pl.BlockSpec((1, d), lambda i, f: (0, 0)),
            pl.BlockSpec((None, MOD_ROWS, d), lambda i, f: (i // tiles_per_mod, 0, 0)),
            pl.BlockSpec((d, tf), lambda i, f: (0, f)),
            pl.BlockSpec((d, tf), lambda i, f: (0, f)),
            pl.BlockSpec((tf, d), lambda i, f: (f, 0)),
        ],
        out_specs=pl.BlockSpec((tm, d), lambda i, f: (i, 0)),
        scratch_shapes=[pltpu.VMEM((tm, d), BF16), pltpu.VMEM((tm, d), F32)],
        compiler_params=_params("arbitrary", "arbitrary"),
        name="ffn_dense",
    )(x, g.reshape(1, d), mod, w1, w3, w2)


def _router_kernel(x_ref, g_ref, mod_ref, wr_ref, h_ref, r_ref, *, n_experts):
    h = _norm_mod(x_ref[...], g_ref[...], mod_ref[...], 3, 4)
    h_ref[...] = h
    logits = jnp.dot(h.astype(BF16), wr_ref[...], preferred_element_type=F32)
    lane = lax.broadcasted_iota(jnp.int32, logits.shape, 1)
    neg = jnp.float32(-jnp.inf)
    logits = jnp.where(lane < n_experts, logits, neg)
    m1 = jnp.max(logits, axis=-1, keepdims=True)
    i1 = jnp.min(jnp.where(logits == m1, lane, LANES), axis=-1, keepdims=True)
    rest = jnp.where(lane == i1, neg, logits)
    m2 = jnp.max(rest, axis=-1, keepdims=True)
    i2 = jnp.min(jnp.where(rest == m2, lane, LANES), axis=-1, keepdims=True)
    e2 = jnp.exp(m2 - m1)
    den = 1.0 + e2
    r = jnp.where(lane == 0, i1.astype(F32), 0.0)
    r = jnp.where(lane == 1, i2.astype(F32), r)
    r = jnp.where(lane == 2, 1.0 / den, r)
    r = jnp.where(lane == 3, e2 / den, r)
    r_ref[...] = r


def _router(x, g, mod, w_router, rows_per_mod):
    t, d = x.shape
    n_experts = w_router.shape[1]
    wr = jnp.zeros((d, LANES), BF16).at[:, :n_experts].set(w_router.astype(BF16))
    tm = _tile(rows_per_mod, 512)
    tiles_per_mod = rows_per_mod // tm
    return pl.pallas_call(
        functools.partial(_router_kernel, n_experts=n_experts),
        out_shape=(jax.ShapeDtypeStruct((t, d), F32), jax.ShapeDtypeStruct((t, LANES), F32)),
        grid=(t // tm,),
        in_specs=[
            pl.BlockSpec((tm, d), lambda i: (i, 0)),
            pl.BlockSpec((1, d), lambda i: (0, 0)),
            pl.BlockSpec((None, MOD_ROWS, d), lambda i: (i // tiles_per_mod, 0, 0)),
            pl.BlockSpec((d, LANES), lambda i: (0, 0)),
        ],
        out_specs=(pl.BlockSpec((tm, d), lambda i: (i, 0)), pl.BlockSpec((tm, LANES), lambda i: (i, 0))),
        compiler_params=_params("arbitrary"),
        name="moe_router",
    )(x, g.reshape(1, d), mod, wr)


GATHER_UNROLL = 8


def _row_copy(idx_ref, src_hbm, dst_ref, sem, r):
    return pltpu.make_async_copy(src_hbm.at[pl.ds(idx_ref[r], 1), :], dst_ref.at[pl.ds(r, 1), :], sem)


def _gather_start(idx_ref, src_hbm, dst_ref, sem, n_rows):
    def body(r, c):
        _row_copy(idx_ref, src_hbm, dst_ref, sem, r).start()
        return c

    lax.fori_loop(0, n_rows, body, 0, unroll=GATHER_UNROLL)


def _gather_wait(idx_ref, src_hbm, dst_ref, sem, n_rows):
    def body(r, c):
        _row_copy(idx_ref, src_hbm, dst_ref, sem, r).wait()
        return c

    lax.fori_loop(0, n_rows, body, 0, unroll=GATHER_UNROLL)


def _moe_kernel(be_ref, nu_ref, tok_ref, tok_next_ref, h_hbm, w1_ref, w3_ref, w2_ref, o_ref,
                xs_scr, hb_scr, acc_scr, sem, *, rows):
    i = pl.program_id(0)
    f = pl.program_id(1)
    n_used = nu_ref[0]
    used = i < n_used

    @pl.when(jnp.logical_and(used, jnp.logical_and(i == 0, f == 0)))
    def _():
        _gather_start(tok_ref.at[0, 0], h_hbm, xs_scr, sem, rows)

    @pl.when(jnp.logical_and(used, f == 0))
    def _():
        _gather_wait(tok_ref.at[0, 0], h_hbm, xs_scr, sem, rows)
        hb_scr[...] = xs_scr[...].astype(BF16)
        acc_scr[...] = jnp.zeros_like(acc_scr)

    @pl.when(jnp.logical_and(i + 1 < n_used, f == 1))
    def _():
        _gather_start(tok_next_ref.at[0, 0], h_hbm, xs_scr, sem, rows)

    @pl.when(used)
    def _():
        acc_scr[...] += _swiglu_tile(hb_scr[...], w1_ref, w3_ref, w2_ref)

    @pl.when(f == pl.num_programs(1) - 1)
    def _():
        @pl.when(used)
        def _():
            o_ref[...] = acc_scr[...]

        @pl.when(jnp.logical_not(used))
        def _():
            o_ref[...] = jnp.zeros_like(o_ref)


def _moe_ffn(h, slot_tok, block_e, n_used, w1, w3, w2):
    d = h.shape[1]
    dff = w1.shape[2]
    n_slots = slot_tok.shape[0]
    rows = MOE_ROWS
    nb = n_slots // rows
    tf = _tile(dff, 512)
    nf = dff // tf
    assert nf >= 2

    def wcol(i, f, be, nu):
        return (be[i], 0, jnp.where(i < nu[0], f, nf - 1))

    def wrow(i, f, be, nu):
        return (be[i], jnp.where(i < nu[0], f, nf - 1), 0)

    tok = slot_tok.reshape(nb, 1, rows)
    return pl.pallas_call(
        functools.partial(_moe_kernel, rows=rows),
        out_shape=jax.ShapeDtypeStruct((n_slots, d), F32),
        grid_spec=pltpu.PrefetchScalarGridSpec(
            num_scalar_prefetch=2,
            grid=(nb, nf),
            in_specs=[
                pl.BlockSpec((1, 1, rows), lambda i, f, be, nu: (i, 0, 0), memory_space=pltpu.SMEM),
                pl.BlockSpec((1, 1, rows), lambda i, f, be, nu: (jnp.minimum(i + 1, nb - 1), 0, 0),
                             memory_space=pltpu.SMEM),
                pl.BlockSpec(memory_space=pl.ANY),
                pl.BlockSpec((None, d, tf), wcol),
                pl.BlockSpec((None, d, tf), wcol),
                pl.BlockSpec((None, tf, d), wrow),
            ],
            out_specs=pl.BlockSpec((rows, d), lambda i, f, be, nu: (i, 0)),
            scratch_shapes=[pltpu.VMEM((rows, d), F32), pltpu.VMEM((rows, d), BF16),
                            pltpu.VMEM((rows, d), F32), pltpu.SemaphoreType.DMA],
        ),
        compiler_params=_params("arbitrary", "arbitrary"),
        name="moe_ffn",
    )(block_e, n_used, tok, tok, h, w1, w3, w2)


def _moe_combine_kernel(s0_ref, s1_ref, s0n_ref, s1n_ref, r_ref, x_ref, mod_ref, ys_hbm, o_ref, y_scr, sem,
                        *, rows):
    i = pl.program_id(0)
    cur = i % 2

    def start(idx0_ref, idx1_ref, buf):
        _gather_start(idx0_ref.at[0, 0], ys_hbm, y_scr.at[buf, 0], sem.at[buf], rows)
        _gather_start(idx1_ref.at[0, 0], ys_hbm, y_scr.at[buf, 1], sem.at[buf], rows)

    @pl.when(i == 0)
    def _():
        start(s0_ref, s1_ref, 0)

    @pl.when(i + 1 < pl.num_programs(0))
    def _():
        start(s0n_ref, s1n_ref, 1 - cur)

    _gather_wait(s0_ref.at[0, 0], ys_hbm, y_scr.at[cur, 0], sem.at[cur], rows)
    _gather_wait(s1_ref.at[0, 0], ys_hbm, y_scr.at[cur, 1], sem.at[cur], rows)
    r = r_ref[...]
    f = y_scr[cur, 0] * r[:, TOP_K:TOP_K + 1] + y_scr[cur, 1] * r[:, TOP_K + 1:TOP_K + 2]
    o_ref[...] = x_ref[...] + mod_ref[5:6, :] * f


def _moe_combine(x, mod, ys, route, slot0, slot1, rows_per_mod):
    t, d = x.shape
    tm = _tile(rows_per_mod, 256)
    nt = t // tm
    tiles_per_mod = rows_per_mod // tm
    cur_spec = pl.BlockSpec((1, 1, tm), lambda i: (i, 0, 0), memory_space=pltpu.SMEM)
    next_spec = pl.BlockSpec((1, 1, tm), lambda i: (jnp.minimum(i + 1, nt - 1), 0, 0), memory_space=pltpu.SMEM)
    s0 = slot0.reshape(nt, 1, tm)
    s1 = slot1.reshape(nt, 1, tm)
    return pl.pallas_call(
        functools.partial(_moe_combine_kernel, rows=tm),
        out_shape=jax.ShapeDtypeStruct((t, d), F32),
        grid=(nt,),
        in_specs=[
            cur_spec, cur_spec, next_spec, next_spec,
            pl.BlockSpec((tm, LANES), lambda i: (i, 0)),
            pl.BlockSpec((tm, d), lambda i: (i, 0)),
            pl.BlockSpec((None, MOD_ROWS, d), lambda i: (i // tiles_per_mod, 0, 0)),
            pl.BlockSpec(memory_space=pl.ANY),
        ],
        out_specs=pl.BlockSpec((tm, d), lambda i: (i, 0)),
        scratch_shapes=[pltpu.VMEM((2, TOP_K, tm, d), F32), pltpu.SemaphoreType.DMA((2,))],
        compiler_params=_params("arbitrary"),
        name="moe_combine",
    )(s0, s1, s0, s1, route, x, mod, ys)


def _moe(x, g, mod, w_router, w1, w3, w2, rows_per_mod):
    t, d = x.shape
    n_experts = w_router.shape[1]
    h, r = _router(x, g, mod, w_router, rows_per_mod)
    e_flat = r[:, :TOP_K].astype(jnp.int32).reshape(-1)
    n_assign = t * TOP_K
    onehot = (e_flat[:, None] == jnp.arange(n_experts, dtype=jnp.int32)[None, :]).astype(jnp.int32)
    csum = jnp.cumsum(onehot, axis=0)
    rank = jnp.take_along_axis(csum, e_flat[:, None], axis=1)[:, 0] - 1
    counts = csum[-1]
    padded = (counts + MOE_ROWS - 1) // MOE_ROWS * MOE_ROWS
    pad_end = jnp.cumsum(padded)
    pad_start = pad_end - padded
    slot = (pad_start[e_flat] + rank).astype(jnp.int32)
    nb = -(-n_assign // MOE_ROWS) + n_experts
    n_slots = nb * MOE_ROWS
    tok = jnp.arange(n_assign, dtype=jnp.int32) // TOP_K
    slot_tok = jnp.zeros((n_slots,), jnp.int32).at[slot].set(tok, unique_indices=True)
    block_start = jnp.arange(nb, dtype=jnp.int32) * MOE_ROWS
    block_e = jnp.minimum(jnp.sum(block_start[:, None] >= pad_end[None, :], axis=1), n_experts - 1).astype(jnp.int32)
    n_used = (pad_end[-1:] // MOE_ROWS).astype(jnp.int32)
    ys = _moe_ffn(h, slot_tok, block_e, n_used, w1, w3, w2)
    slot2 = slot.reshape(t, TOP_K)
    return _moe_combine(x, mod, ys, r, slot2[:, 0], slot2[:, 1], rows_per_mod)


def _final_norm_kernel(x_ref, g_ref, o_ref):
    x = x_ref[...]
    o_ref[...] = x * lax.rsqrt(jnp.mean(x * x, axis=-1, keepdims=True) + EPS) * g_ref[...]


def _final_norm(x, g):
    t, d = x.shape
    tm = _tile(t, 1024)
    return pl.pallas_call(
        _final_norm_kernel,
        out_shape=jax.ShapeDtypeStruct((t, d), F32),
        grid=(t // tm,),
        in_specs=[pl.BlockSpec((tm, d), lambda i: (i, 0)), pl.BlockSpec((1, d), lambda i: (0, 0))],
        out_specs=pl.BlockSpec((tm, d), lambda i: (i, 0)),
        compiler_params=_params("arbitrary"),
        name="final_norm",
    )(x, g.reshape(1, d))


def kernel(x_prompt, x_sample, cache_k, cache_v, state_ssm_re, state_ssm_im, c, c_ctx, w_mod, b_mod, norm1_g, norm2_g, w_in, w_out, lam_q1, lam_k1, lam_q2, lam_k2, head_g, ssm_a_re, ssm_a_im, ssm_log_dt, ssm_b_re, ssm_b_im, ssm_c_re, ssm_c_im, ssm_d, w_glu, ssm_norm_g, ffn_w1, ffn_w3, ffn_w2, moe_router, moe_w1, moe_w3, moe_w2, final_g):
    batch, seq, d = x_prompt.shape
    dec_batch, dec_seq, _ = x_sample.shape
    depth = w_mod.shape[0]
    _, _, n_past, n_heads, _, qk_dim = cache_k.shape
    v_dim = cache_v.shape[-1]
    n_groups, n_state = state_ssm_re.shape[-2:]
    ssm_w = w_glu.shape[-1]
    qk_w = n_heads * 2 * qk_dim
    attn_w = n_heads * v_dim
    u_off = 2 * qk_w + attn_w
    assert u_off % ssm_w == 0 and ssm_w % LANES == 0

    n_cond = 1 + dec_batch
    cond = jnp.zeros((-(-n_cond // 8) * 8, d), F32).at[0].set(c_ctx).at[1:n_cond].set(c)
    mods = _adaln(cond, w_mod, b_mod)[:, :n_cond].reshape(depth, n_cond, N_MOD, d)
    mods = jnp.pad(mods, ((0, 0), (0, 0), (0, MOD_ROWS - N_MOD), (0, 0)))

    cos, sin = _rope_tables(dec_seq, qk_dim)
    ck = cache_k.reshape(dec_batch, depth, n_past, qk_w)
    cv = cache_v.reshape(dec_batch, depth, n_past, attn_w)

    def to_groups(s):
        return jnp.transpose(s, (2, 0, 1, 3)).reshape(n_groups, s.shape[0], 2 * n_state)

    xp = x_prompt.reshape(batch * seq, d)
    xs = x_sample.reshape(dec_batch * dec_seq, d)
    zeros_state = jnp.zeros((n_groups, batch, 2 * n_state), F32)
    new_k, new_v, new_re, new_im = [], [], [], []
    for l in range(depth):
        w_in_l = w_in[l].astype(BF16)
        w_out_a = w_out[l, :attn_w].astype(BF16)
        w_out_s = w_out[l, attn_w:].astype(BF16)
        w_glu_l = w_glu[l].astype(BF16)
        lam = (jnp.exp(jnp.sum(lam_q1[l] * lam_k1[l])) - jnp.exp(jnp.sum(lam_q2[l] * lam_k2[l]))
               + _lambda_init(l)).reshape(1).astype(F32)
        mats = _ssm_matrices(ssm_a_re[l], ssm_a_im[l], ssm_log_dt[l], ssm_b_re[l], ssm_b_im[l],
                             ssm_c_re[l], ssm_c_im[l])
        if l % 2 == 0:
            ffn_w = (ffn_w1[l // 2].astype(BF16), ffn_w3[l // 2].astype(BF16), ffn_w2[l // 2].astype(BF16))
        else:
            ffn_w = (moe_w1[l // 2].astype(BF16), moe_w3[l // 2].astype(BF16), moe_w2[l // 2].astype(BF16))

        def layer(x, mod, b, s, rows_per_mod, past, tabs, h0re, h0im):
            z = _in_proj(x, norm1_g[l], mod, w_in_l, rows_per_mod)
            attn = _attention(z, lam, head_g[l], l, b, s, n_heads, qk_dim, v_dim, past, tabs)
            y, fre, fim = _ssm(_ssm_pack(z, u_off // ssm_w, ssm_w, n_groups, b, s), mats, h0re, h0im, b, s)
            ssm = _ssm_post(y, z, u_off // ssm_w, ssm_d[l], w_glu_l, ssm_norm_g[l], b, s)
            x = _out_proj(x, attn, ssm, w_out_a, w_out_s, mod, rows_per_mod)
            if l % 2 == 0:
                x = _ffn(x, norm2_g[l], mod, *ffn_w, rows_per_mod)
            else:
                x = _moe(x, norm2_g[l], mod, moe_router[l // 2], *ffn_w, rows_per_mod)
            return x, z, fre, fim

        xp, zp, fre, fim = layer(xp, mods[l, :1], batch, seq, batch * seq, None, None, zeros_state, zeros_state)
        new_k.append(zp[:, qk_w:2 * qk_w].reshape(batch, seq, n_heads, 2, qk_dim))
        new_v.append(zp[:, 2 * qk_w:u_off].reshape(batch, seq, n_heads, v_dim))
        new_re.append(jnp.transpose(fre.reshape(n_groups, batch, 2, n_state), (1, 2, 0, 3)))
        new_im.append(jnp.transpose(fim.reshape(n_groups, batch, 2, n_state), (1, 2, 0, 3)))
        xs, _, _, _ = layer(xs, mods[l, 1:], dec_batch, dec_seq, dec_seq, (ck, cv), (cos, sin),
                            to_groups(state_ssm_re[:, l]), to_groups(state_ssm_im[:, l]))

    y_prompt = _final_norm(xp, final_g).reshape(batch, seq, d)
    y_sample = _final_norm(xs, final_g).reshape(dec_batch, dec_seq, d)
    return (y_prompt, y_sample, jnp.stack(new_k, axis=1), jnp.stack(new_v, axis=1),
            jnp.stack(new_re, axis=1), jnp.stack(new_im, axis=1))
```

```python
import functools
import math

import jax
import jax.numpy as jnp
from jax import lax
from jax.experimental import pallas as pl
from jax.experimental.pallas import tpu as pltpu

F32 = jnp.float32
BF16 = jnp.bfloat16

EPS = 1e-6
GRID_W = 64
ROPE_BASE = 10000.0
TOP_K = 2
SSM_CHUNK = 16
MOE_ROWS = 512
N_MOD = 6
MOD_ROWS = 8
LANES = 128
FFN_PIECE = 256
SCORE_LEAD = 5
VMEM_LIMIT_BYTES = 56 * 1024 * 1024


def _lambda_init(l):
    return 0.8 - 0.6 * math.exp(-0.3 * l)


def _tile(dim, pref):
    if dim <= pref:
        return dim
    t = pref
    while dim % t:
        t -= 8
    assert t > 0
    return t


def _params(*sem):
    return pltpu.CompilerParams(dimension_semantics=sem, vmem_limit_bytes=VMEM_LIMIT_BYTES)


def _sigmoid(x):
    return 1.0 / (1.0 + jnp.exp(-x))


def _norm_mod(x, g, mod, shift_row, scale_row):
    y = x * lax.rsqrt(jnp.mean(x * x, axis=-1, keepdims=True) + EPS) * g
    return y * (1.0 + mod[scale_row:scale_row + 1, :]) + mod[shift_row:shift_row + 1, :]


def _adaln_kernel(c_ref, w_ref, b_ref, o_ref):
    c = c_ref[...]
    s = (c * _sigmoid(c)).astype(BF16)
    o_ref[...] = jnp.dot(s, w_ref[...].astype(BF16), preferred_element_type=F32) + b_ref[...]


def _adaln(cond, w_mod, b_mod):
    depth, d, n = w_mod.shape
    r = cond.shape[0]
    tn = _tile(n, 1024)
    return pl.pallas_call(
        _adaln_kernel,
        out_shape=jax.ShapeDtypeStruct((depth, r, n), F32),
        grid=(depth, n // tn),
        in_specs=[
            pl.BlockSpec((r, d), lambda l, j: (0, 0)),
            pl.BlockSpec((None, d, tn), lambda l, j: (l, 0, j)),
            pl.BlockSpec((None, 1, tn), lambda l, j: (l, 0, j)),
        ],
        out_specs=pl.BlockSpec((None, r, tn), lambda l, j: (l, 0, j)),
        compiler_params=_params("arbitrary", "arbitrary"),
        name="adaln",
    )(cond, w_mod, b_mod.reshape(depth, 1, n))


def _rope(x, cos, sin_signed):
    width = x.shape[1]
    lane = lax.broadcasted_iota(jnp.int32, x.shape, 1)
    first_half = (lane % 32) < 16
    partner = jnp.where(first_half, pltpu.roll(x, width - 16, 1), pltpu.roll(x, 16, 1))
    return x * cos + partner * sin_signed


def _in_proj_kernel(*refs, with_kv, n_alias, q_tiles, kv_tiles):
    x_ref, g_ref, mod_ref, w_ref = refs[:4]
    refs = refs[4 + n_alias:]
    if with_kv:
        z_ref, u_ref, k_ref, v_ref, h_scr = refs
    else:
        z_ref, u_ref, h_scr = refs
    j = pl.program_id(1)

    @pl.when(j == 0)
    def _():
        h_scr[...] = _norm_mod(x_ref[...], g_ref[...], mod_ref[...], 0, 1).astype(BF16)

    r = jnp.dot(h_scr[...], w_ref[...], preferred_element_type=F32)
    z_ref[...] = r.astype(BF16)

    @pl.when(j == 0)
    def _():
        u_ref[...] = r

    if with_kv:
        @pl.when(jnp.logical_and(j > q_tiles, j <= q_tiles + kv_tiles))
        def _():
            k_ref[...] = r.reshape(k_ref.shape)

        @pl.when(j > q_tiles + kv_tiles)
        def _():
            v_ref[...] = r.reshape(v_ref.shape)


def _in_proj(x, g, mod, w, rows_per_mod, qk_w, attn_w, ssm_w, kv_out=None):
    t, d = x.shape
    tm = _tile(rows_per_mod, 1024)
    tn = ssm_w
    assert qk_w == attn_w and qk_w % tn == 0
    q_tiles = kv_tiles = qk_w // tn
    qkv_tiles = q_tiles + 2 * kv_tiles
    tiles_per_mod = rows_per_mod // tm
    in_specs = [
        pl.BlockSpec((tm, d), lambda i, j: (i, 0)),
        pl.BlockSpec((1, d), lambda i, j: (0, 0)),
        pl.BlockSpec((None, MOD_ROWS, d), lambda i, j: (i // tiles_per_mod, 0, 0)),
        pl.BlockSpec((d, tn), lambda i, j: (0, j)),
    ]
    args = [x, g.reshape(1, d), mod, w]
    out_shape = [jax.ShapeDtypeStruct((t, qkv_tiles * tn), BF16), jax.ShapeDtypeStruct((t, ssm_w), F32)]
    out_specs = [pl.BlockSpec((tm, tn), lambda i, j: (i, jnp.maximum(j - 1, 0))),
                 pl.BlockSpec((tm, tn), lambda i, j: (i, 0))]
    aliases = {}
    if kv_out is not None:
        layer, seq, new_k, new_v = kv_out
        assert tm % seq == 0
        bt = tm // seq
        aliases = {len(args): 2, len(args) + 1: 3}
        in_specs += [pl.BlockSpec(memory_space=pl.ANY)] * 2
        args += [new_k, new_v]
        out_shape += [jax.ShapeDtypeStruct(a.shape, F32) for a in (new_k, new_v)]
        out_specs += [
            pl.BlockSpec((bt, None, seq, tn),
                         lambda i, j: (i, layer, 0, jnp.clip(j - 1 - q_tiles, 0, kv_tiles - 1))),
            pl.BlockSpec((bt, None, seq, tn),
                         lambda i, j: (i, layer, 0, jnp.clip(j - 1 - q_tiles - kv_tiles, 0, kv_tiles - 1))),
        ]
    return pl.pallas_call(
        functools.partial(_in_proj_kernel, with_kv=kv_out is not None, n_alias=len(aliases),
                          q_tiles=q_tiles, kv_tiles=kv_tiles),
        out_shape=tuple(out_shape),
        grid=(t // tm, 1 + qkv_tiles),
        in_specs=in_specs,
        out_specs=tuple(out_specs),
        scratch_shapes=[pltpu.VMEM((tm, d), BF16)],
        input_output_aliases=aliases,
        compiler_params=_params("arbitrary", "arbitrary"),
        name="in_proj",
    )(*args)


ONES_ROWS = 16


def _attn_kernel(*refs, n_past, qk_dim, out_scale, key_chunk):
    if n_past:
        (lam_ref, q_ref, k_ref, v_ref, ck_ref, cv_ref, cq_ref, sq_ref, ckk_ref, skk_ref, hg_ref,
         o_ref, k0_scr, k1_scr, vt_scr) = refs
    else:
        lam_ref, q_ref, k_ref, v_ref, hg_ref, o_ref, k0_scr, k1_scr, vt_scr = refs
    tq = q_ref.shape[0]
    nk = vt_scr.shape[1]
    v_dim = vt_scr.shape[0] - ONES_ROWS

    @pl.when(pl.program_id(2) == 0)
    def _():
        kb = k_ref[...]
        if n_past:
            kb = _rope(kb.astype(F32), ckk_ref[...], skk_ref[...]).astype(BF16)
            ck = ck_ref[...].astype(BF16)
            k0_scr[0:n_past, :] = ck[:, :qk_dim]
            k1_scr[0:n_past, :] = ck[:, qk_dim:]
            vt_scr[0:v_dim, 0:n_past] = cv_ref[...].T.astype(BF16)
        k0_scr[n_past:, :] = kb[:, :qk_dim]
        k1_scr[n_past:, :] = kb[:, qk_dim:]
        vt_scr[0:v_dim, n_past:] = v_ref[...].astype(F32).T.astype(BF16)
        vt_scr[v_dim:, :] = jnp.ones((ONES_ROWS, nk), BF16)

    qb = q_ref[...]
    if n_past:
        qb = _rope(qb.astype(F32), cq_ref[...], sq_ref[...]).astype(BF16)
    nt = (((1,), (1,)), ((), ()))
    qh = (qb[:, :qk_dim], qb[:, qk_dim:])
    k_scr = (k0_scr, k1_scr)
    m = [jnp.full((1, tq), -jnp.inf, F32)] * 2
    acc = [jnp.zeros((v_dim + ONES_ROWS, tq), F32)] * 2
    n_items = 2 * (nk // key_chunk)
    scores = {}
    for t in range(n_items + SCORE_LEAD):
        if t < n_items:
            c, a = divmod(t, 2)
            scores[t] = lax.dot_general(k_scr[a][c * key_chunk:(c + 1) * key_chunk, :], qh[a], nt,
                                        preferred_element_type=F32)
        if t >= SCORE_LEAD:
            c, a = divmod(t - SCORE_LEAD, 2)
            s = scores.pop(t - SCORE_LEAD)
            m_new = jnp.maximum(m[a], jnp.max(s, axis=0, keepdims=True))
            p = jnp.exp2(s - m_new).astype(BF16)
            acc[a] = jnp.exp2(m[a] - m_new) * acc[a] + jnp.dot(
                vt_scr[:, c * key_chunk:(c + 1) * key_chunk], p, preferred_element_type=F32)
            m[a] = m_new
    l0 = acc[0][v_dim:v_dim + 1, :]
    l1 = acc[1][v_dim:v_dim + 1, :]
    o = acc[0][:v_dim, :] * (1.0 / l0) - acc[1][:v_dim, :] * (lam_ref[0] / l1)
    o = o * lax.rsqrt(jnp.mean(o * o, axis=0, keepdims=True) + EPS) * (hg_ref[...] * out_scale)
    o_ref[...] = o.T.astype(o_ref.dtype)


def _attention(z, lam, head_g, layer, batch, seq, n_heads, qk_dim, v_dim, past=None, rope_tabs=None):
    assert 2 * qk_dim == LANES and v_dim == LANES
    t = batch * seq
    tq = _tile(seq, 256)
    nq = seq // tq
    n_past = past[0].shape[2] if past is not None else 0
    nk = n_past + seq
    q_spec = pl.BlockSpec((tq, LANES), lambda b, h, i: (b * nq + i, h))
    k_spec = pl.BlockSpec((seq, LANES), lambda b, h, i: (b, n_heads + h))
    v_spec = pl.BlockSpec((seq, LANES), lambda b, h, i: (b, 2 * n_heads + h))
    smem = pl.BlockSpec(memory_space=pltpu.SMEM)
    hg_spec = pl.BlockSpec((LANES, 1), lambda b, h, i: (0, 0))
    if n_past:
        cache_spec = pl.BlockSpec((None, None, n_past, LANES), lambda b, h, i: (b, layer, 0, h))
        tq_tab = pl.BlockSpec((tq, LANES), lambda b, h, i: (i, 0))
        seq_tab = pl.BlockSpec((seq, LANES), lambda b, h, i: (0, 0))
        cos, sin = rope_tabs
        in_specs = [smem, q_spec, k_spec, v_spec, cache_spec, cache_spec, tq_tab, tq_tab, seq_tab, seq_tab, hg_spec]
        args = (lam, z, z, z, past[0], past[1], cos, sin, cos, sin, head_g.reshape(LANES, 1))
    else:
        in_specs = [smem, q_spec, k_spec, v_spec, hg_spec]
        args = (lam, z, z, z, head_g.reshape(LANES, 1))
    return pl.pallas_call(
        functools.partial(_attn_kernel, n_past=n_past, qk_dim=qk_dim,
                          out_scale=1.0 - _lambda_init(layer), key_chunk=_tile(nk, 256)),
        out_shape=jax.ShapeDtypeStruct((t, n_heads * v_dim), BF16),
        grid=(batch, n_heads, nq),
        in_specs=in_specs,
        out_specs=pl.BlockSpec((tq, LANES), lambda b, h, i: (b * nq + i, h)),
        scratch_shapes=[pltpu.VMEM((nk, qk_dim), BF16), pltpu.VMEM((nk, qk_dim), BF16),
                        pltpu.VMEM((v_dim + ONES_ROWS, nk), BF16)],
        compiler_params=_params("arbitrary", "arbitrary", "arbitrary"),
        name="diff_attn",
    )(*args)


def _rope_tables(seq, qk_dim):
    half = qk_dim // 2
    inv = 1.0 / (ROPE_BASE ** (jnp.arange(0, half, 2, dtype=F32) / half))
    pos = jnp.arange(seq)
    sign = jnp.concatenate([-jnp.ones((half // 2,), F32), jnp.ones((half // 2,), F32)])

    def tab(p):
        ang = p.astype(F32)[:, None] * inv[None, :]
        cos = jnp.concatenate([jnp.cos(ang), jnp.cos(ang)], -1)
        sin = jnp.concatenate([jnp.sin(ang), jnp.sin(ang)], -1) * sign[None, :]
        return cos, sin

    cr, sr = tab(pos // GRID_W)
    cc, sc = tab(pos % GRID_W)
    cos = jnp.concatenate([cr, cc], -1)
    sin = jnp.concatenate([sr, sc], -1)
    reps = LANES // qk_dim
    return jnp.tile(cos, (1, reps)), jnp.tile(sin, (1, reps))


def _ssm_kernel(x_ref, m_ref, bre_ref, bim_ref, cre_ref, cim_ref, are_ref, aim_ref, h0re_ref, h0im_ref,
                y_ref, fre_ref, fim_ref, sre_scr, sim_scr, hfre_scr, hfim_scr, hbre_scr, hbim_scr,
                *, n_chunks, batch, state):
    xb = x_ref[...].astype(BF16)
    sre_scr[...] = jnp.dot(xb, bre_ref[...], preferred_element_type=F32)
    sim_scr[...] = jnp.dot(xb, bim_ref[...], preferred_element_type=F32)
    ar = are_ref[...]
    ai = aim_ref[...]
    fwd_lane = lax.broadcasted_iota(jnp.int32, (batch, 2 * state), 1) < state

    def step(i, carry):
        hre, him = carry
        rf = pl.ds(pl.multiple_of(i * batch, batch), batch)
        rb = pl.ds(pl.multiple_of((n_chunks - 1 - i) * batch, batch), batch)
        hfre_scr[rf, :] = hre
        hfim_scr[rf, :] = him
        hbre_scr[rb, :] = hre
        hbim_scr[rb, :] = him
        sre = jnp.where(fwd_lane, sre_scr[rf, :], sre_scr[rb, :])
        sim = jnp.where(fwd_lane, sim_scr[rf, :], sim_scr[rb, :])
        return ar * hre - ai * him + sre, ar * him + ai * hre + sim

    hre, him = lax.fori_loop(0, n_chunks, step, (h0re_ref[...], h0im_ref[...]))
    fre_ref[...] = hre
    fim_ref[...] = him
    lane = lax.broadcasted_iota(jnp.int32, hfre_scr.shape, 1) < state
    hs_re = jnp.where(lane, hfre_scr[...], hbre_scr[...]).astype(BF16)
    hs_im = jnp.where(lane, hfim_scr[...], hbim_scr[...]).astype(BF16)
    y = jnp.dot(xb, m_ref[...], preferred_element_type=F32)
    y += jnp.dot(hs_re, cre_ref[...], preferred_element_type=F32)
    y += jnp.dot(hs_im, cim_ref[...], preferred_element_type=F32)
    y_ref[...] = y


def _ssm_matrices(a_re, a_im, log_dt, b_re, b_im, c_re, c_im):
    tc = SSM_CHUNK
    hi = lax.Precision.HIGHEST
    g, p = a_re.shape[1:]
    ch = b_re.shape[-1]
    tau = jnp.arange(tc + 1, dtype=F32)
    s_idx = jnp.arange(tc)
    m_tot = 0.0
    b_parts, c_parts, a_parts = [], [], []
    for d in range(2):
        lam = lax.complex(a_re[d].astype(F32), a_im[d].astype(F32))
        lam_dt = lam * jnp.exp(log_dt[d].astype(F32))[:, None]
        a_bar = jnp.exp(lam_dt)
        b_bar = ((a_bar - 1.0) / lam)[..., None] * lax.complex(b_re[d].astype(F32), b_im[d].astype(F32))
        c_mat = lax.complex(c_re[d].astype(F32), c_im[d].astype(F32))
        pw = jnp.exp(tau[:, None, None] * lam_dt[None])
        kern = jnp.einsum('gcp,tgp,gpd->tgcd', c_mat, pw[:tc], b_bar, precision=hi).real
        lag = (s_idx[None, :] - s_idx[:, None]) if d == 0 else (s_idx[:, None] - s_idx[None, :])
        valid = (lag >= 0)
        kk = kern[jnp.clip(lag, 0, tc - 1)]
        kk = jnp.where(valid[:, :, None, None, None], kk, 0.0)
        m_tot = m_tot + jnp.transpose(kk, (2, 0, 4, 1, 3)).reshape(g, tc * ch, tc * ch)
        pw_in = pw[tc - 1 - s_idx] if d == 0 else pw[s_idx]
        bs = pw_in[:, :, :, None] * b_bar[None]
        b_parts.append(jnp.transpose(bs, (1, 0, 3, 2)).reshape(g, tc * ch, p))
        pw_out = pw[s_idx + 1] if d == 0 else pw[tc - s_idx]
        cs = c_mat[None] * pw_out[:, :, None, :]
        c_parts.append(jnp.transpose(cs, (1, 3, 0, 2)).reshape(g, p, tc * ch))
        a_parts.append(pw[tc])
    bcat = jnp.concatenate(b_parts, axis=-1)
    ccat = jnp.concatenate(c_parts, axis=1)
    acat = jnp.concatenate(a_parts, axis=-1)[:, None, :]
    return (m_tot.astype(BF16), bcat.real.astype(BF16), bcat.imag.astype(BF16),
            ccat.real.astype(BF16), (-ccat.imag).astype(BF16), acat.real, acat.imag)


def _chunk_rows(seq):
    nc = seq // SSM_CHUNK
    rb = _tile(nc, 128)
    return nc, rb, nc // rb


def _ssm_pack_kernel(*refs, rb, ch):
    u_refs, x_ref = refs[:-1], refs[-1]
    per_tile = LANES // ch
    slot = lax.broadcasted_iota(jnp.int32, (rb, LANES), 1) // ch
    for j, u_ref in enumerate(u_refs):
        for half in range(SSM_CHUNK // per_tile):
            acc = [None] * per_tile
            for tt in range(per_tile):
                v = u_ref[pl.ds(half * per_tile + tt, rb, stride=SSM_CHUNK), :]
                for gg in range(per_tile):
                    shift = ((tt - gg) % per_tile) * ch
                    r = pltpu.roll(v, shift, 1) if shift else v
                    acc[gg] = r if tt == 0 else jnp.where(slot == tt, r, acc[gg])
            for gg in range(per_tile):
                x_ref[j * per_tile + gg, :, half * LANES:(half + 1) * LANES] = acc[gg]


def _ssm_pack(z, u_col_block, width, n_groups, batch, seq):
    ch = width // n_groups
    kc = SSM_CHUNK * ch
    assert LANES % ch == 0 and SSM_CHUNK % (LANES // ch) == 0 and kc % LANES == 0
    nc, rb, nbk = _chunk_rows(seq)
    n_tiles = width // LANES
    first_tile = u_col_block * n_tiles
    x = pl.pallas_call(
        functools.partial(_ssm_pack_kernel, rb=rb, ch=ch),
        out_shape=jax.ShapeDtypeStruct((n_groups, nc, batch * kc), F32),
        grid=(batch * nbk,),
        in_specs=[pl.BlockSpec((rb * SSM_CHUNK, LANES), functools.partial(lambda i, col: (i, col), col=first_tile + j))
                  for j in range(n_tiles)],
        out_specs=pl.BlockSpec((n_groups, rb, kc), lambda i: (0, i % nbk, i // nbk)),
        compiler_params=_params("arbitrary"),
        name="s5_pack",
    )(*([z] * n_tiles))
    return x.reshape(n_groups, nc * batch, kc)


def _ssm(x, mats, h0re, h0im, batch, seq):
    m, bre, bim, cre, cim, are, aim = mats
    g, kc, _ = m.shape
    p2 = bre.shape[-1]
    nc = seq // SSM_CHUNK
    n = nc * batch
    grp = lambda shape: pl.BlockSpec((None,) + shape, lambda i: (i,) + (0,) * len(shape))
    y, fre, fim = pl.pallas_call(
        functools.partial(_ssm_kernel, n_chunks=nc, batch=batch, state=p2 // 2),
        out_shape=(jax.ShapeDtypeStruct((g, n, kc), F32),
                   jax.ShapeDtypeStruct((g, batch, p2), F32),
                   jax.ShapeDtypeStruct((g, batch, p2), F32)),
        grid=(g,),
        in_specs=[grp((n, kc)), grp((kc, kc)), grp((kc, p2)), grp((kc, p2)), grp((p2, kc)), grp((p2, kc)),
                  grp((1, p2)), grp((1, p2)), grp((batch, p2)), grp((batch, p2))],
        out_specs=(grp((n, kc)), grp((batch, p2)), grp((batch, p2))),
        scratch_shapes=[pltpu.VMEM((n, p2), F32)] * 6,
        compiler_params=_params("arbitrary"),
        name="s5_scan",
    )(x, m, bre, bim, cre, cim, are, aim, h0re, h0im)
    return y, fre, fim


def _ssm_post_kernel(y_ref, u_ref, d_ref, w_ref, g_ref, o_ref, y_scr, *, rb, ch):
    per_tile = LANES // ch
    slot = lax.broadcasted_iota(jnp.int32, (rb, LANES), 1) // ch
    n_tiles = y_scr.shape[0]
    for j in range(n_tiles):
        for half in range(SSM_CHUNK // per_tile):
            for tt in range(per_tile):
                out = None
                for gg in range(per_tile):
                    src = y_ref[j * per_tile + gg, :, half * LANES:(half + 1) * LANES]
                    shift = ((gg - tt) % per_tile) * ch
                    r = pltpu.roll(src, shift, 1) if shift else src
                    out = r if gg == 0 else jnp.where(slot == gg, r, out)
                y_scr[j, pl.ds(half * per_tile + tt, rb, stride=SSM_CHUNK), :] = out
    y = jnp.concatenate([y_scr[j] for j in range(n_tiles)], axis=1) + d_ref[...] * u_ref[...]
    y = 0.5 * y * (1.0 + jnp.tanh(math.sqrt(2.0 / math.pi) * (y + 0.044715 * (y * y * y))))
    y = y * _sigmoid(jnp.dot(y.astype(BF16), w_ref[...], preferred_element_type=F32))
    y = y * lax.rsqrt(jnp.mean(y * y, axis=-1, keepdims=True) + EPS) * g_ref[...]
    o_ref[...] = y.astype(o_ref.dtype)


def _ssm_post(y, z, u_col_block, ssm_d, w_glu, norm_g, batch, seq):
    g, _, kc = y.shape
    w = w_glu.shape[0]
    nc, rb, nbk = _chunk_rows(seq)
    tm = rb * SSM_CHUNK
    fixed = lambda i: (0, 0)
    return pl.pallas_call(
        functools.partial(_ssm_post_kernel, rb=rb, ch=w // g),
        out_shape=jax.ShapeDtypeStruct((batch * seq, w), BF16),
        grid=(batch * nbk,),
        in_specs=[pl.BlockSpec((g, rb, kc), lambda i: (0, i % nbk, i // nbk)),
                  pl.BlockSpec((tm, w), lambda i: (i, u_col_block)),
                  pl.BlockSpec((1, w), fixed), pl.BlockSpec((w, w), fixed), pl.BlockSpec((1, w), fixed)],
        out_specs=pl.BlockSpec((tm, w), lambda i: (i, 0)),
        scratch_shapes=[pltpu.VMEM((w // LANES, tm, LANES), F32)],
        compiler_params=_params("arbitrary"),
        name="s5_post",
    )(y.reshape(g, nc, batch * kc), z, ssm_d.reshape(1, w), w_glu, norm_g.reshape(1, w))


def _out_proj_kernel(x_ref, a_ref, s_ref, wa_ref, ws_ref, mod_ref, o_ref):
    f = jnp.dot(a_ref[...], wa_ref[...], preferred_element_type=F32)
    f += jnp.dot(s_ref[...], ws_ref[...], preferred_element_type=F32)
    o_ref[...] = x_ref[...] + mod_ref[2:3, :] * f


def _out_proj(x, attn, ssm, w_attn, w_ssm, mod, rows_per_mod):
    t, d = x.shape
    ka, ks = attn.shape[1], ssm.shape[1]
    tm = _tile(rows_per_mod, 1024)
    tn = _tile(d, 1024)
    tiles_per_mod = rows_per_mod // tm
    return pl.pallas_call(
        _out_proj_kernel,
        out_shape=jax.ShapeDtypeStruct((t, d), F32),
        grid=(t // tm, d // tn),
        in_specs=[
            pl.BlockSpec((tm, tn), lambda i, j: (i, j)),
            pl.BlockSpec((tm, ka), lambda i, j: (i, 0)),
            pl.BlockSpec((tm, ks), lambda i, j: (i, 0)),
            pl.BlockSpec((ka, tn), lambda i, j: (0, j)),
            pl.BlockSpec((ks, tn), lambda i, j: (0, j)),
            pl.BlockSpec((None, MOD_ROWS, tn), lambda i, j: (i // tiles_per_mod, 0, j)),
        ],
        out_specs=pl.BlockSpec((tm, tn), lambda i, j: (i, j)),
        compiler_params=_params("arbitrary", "arbitrary"),
        name="out_proj",
    )(x, attn, ssm, w_attn, w_ssm, mod)


def _swiglu_tile(h, w1_ref, w3_ref, w2_ref):
    tf = w1_ref.shape[1]
    piece = FFN_PIECE if tf % FFN_PIECE == 0 else tf
    cols = [slice(k, k + piece) for k in range(0, tf, piece)]

    def up(c):
        return (jnp.dot(h, w1_ref[:, c], preferred_element_type=F32),
                jnp.dot(h, w3_ref[:, c], preferred_element_type=F32))

    out = None
    ab = up(cols[0])
    for k, c in enumerate(cols):
        a, b = ab
        if k + 1 < len(cols):
            ab = up(cols[k + 1])
        g = (a * _sigmoid(a) * b).astype(BF16)
        y = jnp.dot(g, w2_ref[c, :], preferred_element_type=F32)
        out = y if out is None else out + y
    return out


def _ffn_kernel(x_ref, g_ref, mod_ref, w1_ref, w3_ref, w2_ref, o_ref, h_scr, acc_scr):
    f = pl.program_id(1)

    @pl.when(f == 0)
    def _():
        h_scr[...] = _norm_mod(x_ref[...], g_ref[...], mod_ref[...], 3, 4).astype(BF16)
        acc_scr[...] = jnp.zeros_like(acc_scr)

    acc_scr[...] += _swiglu_tile(h_scr[...], w1_ref, w3_ref, w2_ref)

    @pl.when(f == pl.num_programs(1) - 1)
    def _():
        o_ref[...] = x_ref[...] + mod_ref[5:6, :] * acc_scr[...]


def _ffn(x, g, mod, w1, w3, w2, rows_per_mod):
    t, d = x.shape
    dff = w1.shape[1]
    tm = _tile(rows_per_mod, 512)
    tf = _tile(dff, 512)
    tiles_per_mod = rows_per_mod // tm
    return pl.pallas_call(
        _ffn_kernel,
        out_shape=jax.ShapeDtypeStruct((t, d), F32),
        grid=(t // tm, dff // tf),
        in_specs=[
            pl.BlockSpec((tm, d), lambda i, f: (i, 0)),
            pl.BlockSpec((1, d), lambda i, f: (0, 0)),
            pl.BlockSpec((None, MOD_ROWS, d), lambda i, f: (i // tiles_per_mod, 0, 0)),
            pl.BlockSpec((d, tf), lambda i, f: (0, f)),
            pl.BlockSpec((d, tf), lambda i, f: (0, f)),
            pl.BlockSpec((tf, d), lambda i, f: (f, 0)),
        ],
        out_specs=pl.BlockSpec((tm, d), lambda i, f: (i, 0)),
        scratch_shapes=[pltpu.VMEM((tm, d), BF16), pltpu.VMEM((tm, d), F32)],
        compiler_params=_params("arbitrary", "arbitrary"),
        name="ffn_dense",
    )(x, g.reshape(1, d), mod, w1, w3, w2)


def _router_kernel(x_ref, g_ref, mod_ref, wr_ref, h_ref, r_ref, *, n_experts):
    h = _norm_mod(x_ref[...], g_ref[...], mod_ref[...], 3, 4)
    h_ref[...] = h
    logits = jnp.dot(h.astype(BF16), wr_ref[...], preferred_element_type=F32)
    lane = lax.broadcasted_iota(jnp.int32, logits.shape, 1)
    neg = jnp.float32(-jnp.inf)
    logits = jnp.where(lane < n_experts, logits, neg)
    m1 = jnp.max(logits, axis=-1, keepdims=True)
    i1 = jnp.min(jnp.where(logits == m1, lane, LANES), axis=-1, keepdims=True)
    rest = jnp.where(lane == i1, neg, logits)
    m2 = jnp.max(rest, axis=-1, keepdims=True)
    i2 = jnp.min(jnp.where(rest == m2, lane, LANES), axis=-1, keepdims=True)
    e2 = jnp.exp(m2 - m1)
    den = 1.0 + e2
    r = jnp.where(lane == 0, i1.astype(F32), 0.0)
    r = jnp.where(lane == 1, i2.astype(F32), r)
    r = jnp.where(lane == 2, 1.0 / den, r)
    r = jnp.where(lane == 3, e2 / den, r)
    r_ref[...] = r


def _router(x, g, mod, w_router, rows_per_mod):
    t, d = x.shape
    n_experts = w_router.shape[1]
    wr = jnp.zeros((d, LANES), BF16).at[:, :n_experts].set(w_router.astype(BF16))
    tm = _tile(rows_per_mod, 512)
    tiles_per_mod = rows_per_mod // tm
    return pl.pallas_call(
        functools.partial(_router_kernel, n_experts=n_experts),
        out_shape=(jax.ShapeDtypeStruct((t, d), F32), jax.ShapeDtypeStruct((t, LANES), F32)),
        grid=(t // tm,),
        in_specs=[
            pl.BlockSpec((tm, d), lambda i: (i, 0)),
            pl.BlockSpec((1, d), lambda i: (0, 0)),
            pl.BlockSpec((None, MOD_ROWS, d), lambda i: (i // tiles_per_mod, 0, 0)),
            pl.BlockSpec((d, LANES), lambda i: (0, 0)),
        ],
        out_specs=(pl.BlockSpec((tm, d), lambda i: (i, 0)), pl.BlockSpec((tm, LANES), lambda i: (i, 0))),
        compiler_params=_params("arbitrary"),
        name="moe_router",
    )(x, g.reshape(1, d), mod, wr)


GATHER_UNROLL = 8


def _row_copy(idx_ref, src_hbm, dst_ref, sem, r):
    return pltpu.make_async_copy(src_hbm.at[pl.ds(idx_ref[r], 1), :], dst_ref.at[pl.ds(r, 1), :], sem)


def _gather_start(idx_ref, src_hbm, dst_ref, sem, n_rows):
    def body(r, c):
        _row_copy(idx_ref, src_hbm, dst_ref, sem, r).start()
        return c

    lax.fori_loop(0, n_rows, body, 0, unroll=GATHER_UNROLL)


def _gather_wait(idx_ref, src_hbm, dst_ref, sem, n_rows):
    def body(r, c):
        _row_copy(idx_ref, src_hbm, dst_ref, sem, r).wait()
        return c

    lax.fori_loop(0, n_rows, body, 0, unroll=GATHER_UNROLL)


def _moe_kernel(be_ref, nu_ref, tok_ref, tok_next_ref, h_hbm, w1_ref, w3_ref, w2_ref, o_ref,
                xs_scr, hb_scr, acc_scr, sem, *, rows):
    i = pl.program_id(0)
    f = pl.program_id(1)
    n_used = nu_ref[0]
    used = i < n_used

    @pl.when(jnp.logical_and(used, jnp.logical_and(i == 0, f == 0)))
    def _():
        _gather_start(tok_ref.at[0, 0], h_hbm, xs_scr, sem, rows)

    @pl.when(jnp.logical_and(used, f == 0))
    def _():
        _gather_wait(tok_ref.at[0, 0], h_hbm, xs_scr, sem, rows)
        hb_scr[...] = xs_scr[...].astype(BF16)
        acc_scr[...] = jnp.zeros_like(acc_scr)

    @pl.when(jnp.logical_and(i + 1 < n_used, f == 1))
    def _():
        _gather_start(tok_next_ref.at[0, 0], h_hbm, xs_scr, sem, rows)

    @pl.when(used)
    def _():
        acc_scr[...] += _swiglu_tile(hb_scr[...], w1_ref, w3_ref, w2_ref)

    @pl.when(f == pl.num_programs(1) - 1)
    def _():
        @pl.when(used)
        def _():
            o_ref[...] = acc_scr[...]

        @pl.when(jnp.logical_not(used))
        def _():
            o_ref[...] = jnp.zeros_like(o_ref)


def _moe_ffn(h, slot_tok, block_e, n_used, w1, w3, w2):
    d = h.shape[1]
    dff = w1.shape[2]
    n_slots = slot_tok.shape[0]
    rows = MOE_ROWS
    nb = n_slots // rows
    tf = _tile(dff, 512)
    nf = dff // tf
    assert nf >= 2

    def wcol(i, f, be, nu):
        return (be[i], 0, jnp.where(i < nu[0], f, nf - 1))

    def wrow(i, f, be, nu):
        return (be[i], jnp.where(i < nu[0], f, nf - 1), 0)

    tok = slot_tok.reshape(nb, 1, rows)
    return pl.pallas_call(
        functools.partial(_moe_kernel, rows=rows),
        out_shape=jax.ShapeDtypeStruct((n_slots, d), F32),
        grid_spec=pltpu.PrefetchScalarGridSpec(
            num_scalar_prefetch=2,
            grid=(nb, nf),
            in_specs=[
                pl.BlockSpec((1, 1, rows), lambda i, f, be, nu: (i, 0, 0), memory_space=pltpu.SMEM),
                pl.BlockSpec((1, 1, rows), lambda i, f, be, nu: (jnp.minimum(i + 1, nb - 1), 0, 0),
                             memory_space=pltpu.SMEM),
                pl.BlockSpec(memory_space=pl.ANY),
                pl.BlockSpec((None, d, tf), wcol),
                pl.BlockSpec((None, d, tf), wcol),
                pl.BlockSpec((None, tf, d), wrow),
            ],
            out_specs=pl.BlockSpec((rows, d), lambda i, f, be, nu: (i, 0)),
            scratch_shapes=[pltpu.VMEM((rows, d), F32), pltpu.VMEM((rows, d), BF16),
                            pltpu.VMEM((rows, d), F32), pltpu.SemaphoreType.DMA],
        ),
        compiler_params=_params("arbitrary", "arbitrary"),
        name="moe_ffn",
    )(block_e, n_used, tok, tok, h, w1, w3, w2)


def _moe_combine_kernel(s0_ref, s1_ref, s0n_ref, s1n_ref, r_ref, x_ref, mod_ref, ys_hbm, o_ref, y_scr, sem,
                        *, rows):
    i = pl.program_id(0)
    cur = i % 2

    def start(idx0_ref, idx1_ref, buf):
        _gather_start(idx0_ref.at[0, 0], ys_hbm, y_scr.at[buf, 0], sem.at[buf], rows)
        _gather_start(idx1_ref.at[0, 0], ys_hbm, y_scr.at[buf, 1], sem.at[buf], rows)

    @pl.when(i == 0)
    def _():
        start(s0_ref, s1_ref, 0)

    @pl.when(i + 1 < pl.num_programs(0))
    def _():
        start(s0n_ref, s1n_ref, 1 - cur)

    _gather_wait(s0_ref.at[0, 0], ys_hbm, y_scr.at[cur, 0], sem.at[cur], rows)
    _gather_wait(s1_ref.at[0, 0], ys_hbm, y_scr.at[cur, 1], sem.at[cur], rows)
    r = r_ref[...]
    f = y_scr[cur, 0] * r[:, TOP_K:TOP_K + 1] + y_scr[cur, 1] * r[:, TOP_K + 1:TOP_K + 2]
    o_ref[...] = x_ref[...] + mod_ref[5:6, :] * f


def _moe_combine(x, mod, ys, route, slot0, slot1, rows_per_mod):
    t, d = x.shape
    tm = _tile(rows_per_mod, 256)
    nt = t // tm
    tiles_per_mod = rows_per_mod // tm
    cur_spec = pl.BlockSpec((1, 1, tm), lambda i: (i, 0, 0), memory_space=pltpu.SMEM)
    next_spec = pl.BlockSpec((1, 1, tm), lambda i: (jnp.minimum(i + 1, nt - 1), 0, 0), memory_space=pltpu.SMEM)
    s0 = slot0.reshape(nt, 1, tm)
    s1 = slot1.reshape(nt, 1, tm)
    return pl.pallas_call(
        functools.partial(_moe_combine_kernel, rows=tm),
        out_shape=jax.ShapeDtypeStruct((t, d), F32),
        grid=(nt,),
        in_specs=[
            cur_spec, cur_spec, next_spec, next_spec,
            pl.BlockSpec((tm, LANES), lambda i: (i, 0)),
            pl.BlockSpec((tm, d), lambda i: (i, 0)),
            pl.BlockSpec((None, MOD_ROWS, d), lambda i: (i // tiles_per_mod, 0, 0)),
            pl.BlockSpec(memory_space=pl.ANY),
        ],
        out_specs=pl.BlockSpec((tm, d), lambda i: (i, 0)),
        scratch_shapes=[pltpu.VMEM((2, TOP_K, tm, d), F32), pltpu.SemaphoreType.DMA((2,))],
        compiler_params=_params("arbitrary"),
        name="moe_combine",
    )(s0, s1, s0, s1, route, x, mod, ys)


def _moe(x, g, mod, w_router, w1, w3, w2, rows_per_mod):
    t, d = x.shape
    n_experts = w_router.shape[1]
    h, r = _router(x, g, mod, w_router, rows_per_mod)
    e_flat = r[:, :TOP_K].astype(jnp.int32).reshape(-1)
    n_assign = t * TOP_K
    onehot = (e_flat[:, None] == jnp.arange(n_experts, dtype=jnp.int32)[None, :]).astype(jnp.int32)
    csum = jnp.cumsum(onehot, axis=0)
    rank = jnp.take_along_axis(csum, e_flat[:, None], axis=1)[:, 0] - 1
    counts = csum[-1]
    padded = (counts + MOE_ROWS - 1) // MOE_ROWS * MOE_ROWS
    pad_end = jnp.cumsum(padded)
    pad_start = pad_end - padded
    slot = (pad_start[e_flat] + rank).astype(jnp.int32)
    nb = -(-n_assign // MOE_ROWS) + n_experts
    n_slots = nb * MOE_ROWS
    tok = jnp.arange(n_assign, dtype=jnp.int32) // TOP_K
    slot_tok = jnp.zeros((n_slots,), jnp.int32).at[slot].set(tok, unique_indices=True)
    block_start = jnp.arange(nb, dtype=jnp.int32) * MOE_ROWS
    block_e = jnp.minimum(jnp.sum(block_start[:, None] >= pad_end[None, :], axis=1), n_experts - 1).astype(jnp.int32)
    n_used = (pad_end[-1:] // MOE_ROWS).astype(jnp.int32)
    ys = _moe_ffn(h, slot_tok, block_e, n_used, w1, w3, w2)
    slot2 = slot.reshape(t, TOP_K)
    return _moe_combine(x, mod, ys, r, slot2[:, 0], slot2[:, 1], rows_per_mod)


def _final_norm_kernel(x_ref, g_ref, o_ref):
    x = x_ref[...]
    o_ref[...] = x * lax.rsqrt(jnp.mean(x * x, axis=-1, keepdims=True) + EPS) * g_ref[...]


def _final_norm(x, g):
    t, d = x.shape
    tm = _tile(t, 1024)
    return pl.pallas_call(
        _final_norm_kernel,
        out_shape=jax.ShapeDtypeStruct((t, d), F32),
        grid=(t // tm,),
        in_specs=[pl.BlockSpec((tm, d), lambda i: (i, 0)), pl.BlockSpec((1, d), lambda i: (0, 0))],
        out_specs=pl.BlockSpec((tm, d), lambda i: (i, 0)),
        compiler_params=_params("arbitrary"),
        name="final_norm",
    )(x, g.reshape(1, d))


def kernel(x_prompt, x_sample, cache_k, cache_v, state_ssm_re, state_ssm_im, c, c_ctx, w_mod, b_mod, norm1_g, norm2_g, w_in, w_out, lam_q1, lam_k1, lam_q2, lam_k2, head_g, ssm_a_re, ssm_a_im, ssm_log_dt, ssm_b_re, ssm_b_im, ssm_c_re, ssm_c_im, ssm_d, w_glu, ssm_norm_g, ffn_w1, ffn_w3, ffn_w2, moe_router, moe_w1, moe_w3, moe_w2, final_g):
    batch, seq, d = x_prompt.shape
    dec_batch, dec_seq, _ = x_sample.shape
    depth = w_mod.shape[0]
    _, _, n_past, n_heads, _, qk_dim = cache_k.shape
    v_dim = cache_v.shape[-1]
    n_groups, n_state = state_ssm_re.shape[-2:]
    ssm_w = w_glu.shape[-1]
    qk_w = n_heads * 2 * qk_dim
    attn_w = n_heads * v_dim
    u_off = 2 * qk_w + attn_w
    assert u_off % ssm_w == 0 and ssm_w % LANES == 0

    n_cond = 1 + dec_batch
    cond = jnp.zeros((-(-n_cond // 8) * 8, d), F32).at[0].set(c_ctx).at[1:n_cond].set(c)
    mods = _adaln(cond, w_mod, b_mod)[:, :n_cond].reshape(depth, n_cond, N_MOD, d)
    mods = jnp.pad(mods, ((0, 0), (0, 0), (0, MOD_ROWS - N_MOD), (0, 0)))

    cos, sin = _rope_tables(dec_seq, qk_dim)
    ck = cache_k.reshape(dec_batch, depth, n_past, qk_w)
    cv = cache_v.reshape(dec_batch, depth, n_past, attn_w)

    def to_groups(s):
        return jnp.transpose(s, (2, 0, 1, 3)).reshape(n_groups, s.shape[0], 2 * n_state)

    xp = x_prompt.reshape(batch * seq, d)
    xs = x_sample.reshape(dec_batch * dec_seq, d)
    zeros_state = jnp.zeros((n_groups, batch, 2 * n_state), F32)
    q_scale = qk_dim ** -0.5 * math.log2(math.e)
    new_k = jnp.zeros((batch, depth, seq, qk_w), F32)
    new_v = jnp.zeros((batch, depth, seq, attn_w), F32)
    new_re, new_im = [], []
    for l in range(depth):
        w_in_l = jnp.concatenate([w_in[l, :, u_off:], w_in[l, :, :qk_w] * q_scale, w_in[l, :, qk_w:u_off]],
                                 axis=1).astype(BF16)
        w_out_a = w_out[l, :attn_w].astype(BF16)
        w_out_s = w_out[l, attn_w:].astype(BF16)
        w_glu_l = w_glu[l].astype(BF16)
        lam = (jnp.exp(jnp.sum(lam_q1[l] * lam_k1[l])) - jnp.exp(jnp.sum(lam_q2[l] * lam_k2[l]))
               + _lambda_init(l)).reshape(1).astype(F32)
        mats = _ssm_matrices(ssm_a_re[l], ssm_a_im[l], ssm_log_dt[l], ssm_b_re[l], ssm_b_im[l],
                             ssm_c_re[l], ssm_c_im[l])
        if l % 2 == 0:
            ffn_w = (ffn_w1[l // 2].astype(BF16), ffn_w3[l // 2].astype(BF16), ffn_w2[l // 2].astype(BF16))
        else:
            ffn_w = (moe_w1[l // 2].astype(BF16), moe_w3[l // 2].astype(BF16), moe_w2[l // 2].astype(BF16))

        def layer(x, mod, b, s, rows_per_mod, past, tabs, kv_out, h0re, h0im):
            z, u, *kv = _in_proj(x, norm1_g[l], mod, w_in_l, rows_per_mod, qk_w, attn_w, ssm_w, kv_out)
            attn = _attention(z, lam, head_g[l], l, b, s, n_heads, qk_dim, v_dim, past, tabs)
            y, fre, fim = _ssm(_ssm_pack(u, 0, ssm_w, n_groups, b, s), mats, h0re, h0im, b, s)
            ssm = _ssm_post(y, u, 0, ssm_d[l], w_glu_l, ssm_norm_g[l], b, s)
            x = _out_proj(x, attn, ssm, w_out_a, w_out_s, mod, rows_per_mod)
            if l % 2 == 0:
                x = _ffn(x, norm2_g[l], mod, *ffn_w, rows_per_mod)
            else:
                x = _moe(x, norm2_g[l], mod, moe_router[l // 2], *ffn_w, rows_per_mod)
            return x, kv, fre, fim

        xp, (new_k, new_v), fre, fim = layer(xp, mods[l, :1], batch, seq, batch * seq, None, None,
                                             (l, seq, new_k, new_v), zeros_state, zeros_state)
        new_re.append(jnp.transpose(fre.reshape(n_groups, batch, 2, n_state), (1, 2, 0, 3)))
        new_im.append(jnp.transpose(fim.reshape(n_groups, batch, 2, n_state), (1, 2, 0, 3)))
        xs, _, _, _ = layer(xs, mods[l, 1:], dec_batch, dec_seq, dec_seq, (ck, cv), (cos, sin), None,
                            to_groups(state_ssm_re[:, l]), to_groups(state_ssm_im[:, l]))

    y_prompt = _final_norm(xp, final_g).reshape(batch, seq, d)
    y_sample = _final_norm(xs, final_g).reshape(dec_batch, dec_seq, d)
    return (y_prompt, y_sample, new_k.reshape(batch, depth, seq, n_heads, 2, qk_dim),
            new_v.reshape(batch, depth, seq, n_heads, v_dim), jnp.stack(new_re, axis=1), jnp.stack(new_im, axis=1))
```

```python
import functools
import math

import jax
import jax.numpy as jnp
from jax import lax
from jax.experimental import pallas as pl
from jax.experimental.pallas import tpu as pltpu

F32 = jnp.float32
BF16 = jnp.bfloat16

EPS = 1e-6
GRID_W = 64
ROPE_BASE = 10000.0
TOP_K = 2
SSM_CHUNK = 16
MOE_ROWS = 512
N_MOD = 6
MOD_ROWS = 8
LANES = 128
FFN_PIECE = 256
SCORE_LEAD = 5
VMEM_LIMIT_BYTES = 56 * 1024 * 1024


def _lambda_init(l):
    return 0.8 - 0.6 * math.exp(-0.3 * l)


def _tile(dim, pref):
    if dim <= pref:
        return dim
    t = pref
    while dim % t:
        t -= 8
    assert t > 0
    return t


def _params(*sem):
    return pltpu.CompilerParams(dimension_semantics=sem, vmem_limit_bytes=VMEM_LIMIT_BYTES)


def _sigmoid(x):
    return 1.0 / (1.0 + jnp.exp(-x))


def _norm_mod(x, g, mod, shift_row, scale_row):
    y = x * lax.rsqrt(jnp.mean(x * x, axis=-1, keepdims=True) + EPS) * g
    return y * (1.0 + mod[scale_row:scale_row + 1, :]) + mod[shift_row:shift_row + 1, :]


def _adaln_kernel(c_ref, w_ref, b_ref, o_ref):
    c = c_ref[...]
    s = (c * _sigmoid(c)).astype(BF16)
    o_ref[...] = jnp.dot(s, w_ref[...].astype(BF16), preferred_element_type=F32) + b_ref[...]


def _adaln(cond, w_mod, b_mod):
    depth, d, n = w_mod.shape
    r = cond.shape[0]
    tn = _tile(n, 1024)
    return pl.pallas_call(
        _adaln_kernel,
        out_shape=jax.ShapeDtypeStruct((depth, r, n), F32),
        grid=(depth, n // tn),
        in_specs=[
            pl.BlockSpec((r, d), lambda l, j: (0, 0)),
            pl.BlockSpec((None, d, tn), lambda l, j: (l, 0, j)),
            pl.BlockSpec((None, 1, tn), lambda l, j: (l, 0, j)),
        ],
        out_specs=pl.BlockSpec((None, r, tn), lambda l, j: (l, 0, j)),
        compiler_params=_params("arbitrary", "arbitrary"),
        name="adaln",
    )(cond, w_mod, b_mod.reshape(depth, 1, n))


def _rope(x, cos, sin_signed):
    width = x.shape[1]
    lane = lax.broadcasted_iota(jnp.int32, x.shape, 1)
    first_half = (lane % 32) < 16
    partner = jnp.where(first_half, pltpu.roll(x, width - 16, 1), pltpu.roll(x, 16, 1))
    return x * cos + partner * sin_signed


def _in_proj_kernel(*refs, with_kv, n_alias, q_tiles, kv_tiles):
    x_ref, g_ref, mod_ref, w_ref = refs[:4]
    refs = refs[4 + n_alias:]
    if with_kv:
        z_ref, u_ref, k_ref, v_ref, h_scr = refs
    else:
        z_ref, u_ref, h_scr = refs
    j = pl.program_id(1)

    @pl.when(j == 0)
    def _():
        h_scr[...] = _norm_mod(x_ref[...], g_ref[...], mod_ref[...], 0, 1).astype(BF16)

    r = jnp.dot(h_scr[...], w_ref[...], preferred_element_type=F32)
    z_ref[...] = r.astype(BF16)

    @pl.when(j == 0)
    def _():
        u_ref[...] = r

    if with_kv:
        @pl.when(jnp.logical_and(j > q_tiles, j <= q_tiles + kv_tiles))
        def _():
            k_ref[...] = r.reshape(k_ref.shape)

        @pl.when(j > q_tiles + kv_tiles)
        def _():
            v_ref[...] = r.reshape(v_ref.shape)


def _in_proj(x, g, mod, w, rows_per_mod, qk_w, attn_w, ssm_w, kv_out=None):
    t, d = x.shape
    tm = _tile(rows_per_mod, 1024)
    tn = ssm_w
    assert qk_w == attn_w and qk_w % tn == 0
    q_tiles = kv_tiles = qk_w // tn
    qkv_tiles = q_tiles + 2 * kv_tiles
    tiles_per_mod = rows_per_mod // tm
    in_specs = [
        pl.BlockSpec((tm, d), lambda i, j: (i, 0)),
        pl.BlockSpec((1, d), lambda i, j: (0, 0)),
        pl.BlockSpec((None, MOD_ROWS, d), lambda i, j: (i // tiles_per_mod, 0, 0)),
        pl.BlockSpec((d, tn), lambda i, j: (0, j)),
    ]
    args = [x, g.reshape(1, d), mod, w]
    out_shape = [jax.ShapeDtypeStruct((t, qkv_tiles * tn), BF16), jax.ShapeDtypeStruct((t, ssm_w), F32)]
    out_specs = [pl.BlockSpec((tm, tn), lambda i, j: (i, jnp.maximum(j - 1, 0))),
                 pl.BlockSpec((tm, tn), lambda i, j: (i, 0))]
    aliases = {}
    if kv_out is not None:
        layer, seq, new_k, new_v = kv_out
        assert tm % seq == 0
        bt = tm // seq
        aliases = {len(args): 2, len(args) + 1: 3}
        in_specs += [pl.BlockSpec(memory_space=pl.ANY)] * 2
        args += [new_k, new_v]
        out_shape += [jax.ShapeDtypeStruct(a.shape, F32) for a in (new_k, new_v)]
        out_specs += [
            pl.BlockSpec((bt, None, seq, tn),
                         lambda i, j: (i, layer, 0, jnp.clip(j - 1 - q_tiles, 0, kv_tiles - 1))),
            pl.BlockSpec((bt, None, seq, tn),
                         lambda i, j: (i, layer, 0, jnp.clip(j - 1 - q_tiles - kv_tiles, 0, kv_tiles - 1))),
        ]
    return pl.pallas_call(
        functools.partial(_in_proj_kernel, with_kv=kv_out is not None, n_alias=len(aliases),
                          q_tiles=q_tiles, kv_tiles=kv_tiles),
        out_shape=tuple(out_shape),
        grid=(t // tm, 1 + qkv_tiles),
        in_specs=in_specs,
        out_specs=tuple(out_specs),
        scratch_shapes=[pltpu.VMEM((tm, d), BF16)],
        input_output_aliases=aliases,
        compiler_params=_params("arbitrary", "arbitrary"),
        name="in_proj",
    )(*args)


ONES_ROWS = 16
HEADS_PER_STEP = 2


def _attn_kernel(*refs, n_past, qk_dim, out_scale, key_chunk):
    if n_past:
        (lam_ref, q_ref, k_ref, v_ref, ck_ref, cv_ref, cq_ref, sq_ref, ckk_ref, skk_ref, hg_ref,
         o_ref, k_scr, vt_scr) = refs
    else:
        lam_ref, q_ref, k_ref, v_ref, hg_ref, o_ref, k_scr, vt_scr = refs
    tq = q_ref.shape[0]
    n_heads = q_ref.shape[1] // LANES
    nk = vt_scr.shape[2]
    v_dim = vt_scr.shape[1] - ONES_ROWS
    head_lanes = lambda hh: slice(hh * LANES, (hh + 1) * LANES)

    @pl.when(pl.program_id(2) == 0)
    def _():
        for hh in range(n_heads):
            kb = k_ref[:, head_lanes(hh)]
            if n_past:
                kb = _rope(kb.astype(F32), ckk_ref[...], skk_ref[...]).astype(BF16)
                ck = ck_ref[:, head_lanes(hh)].astype(BF16)
                k_scr[2 * hh, 0:n_past, :] = ck[:, :qk_dim]
                k_scr[2 * hh + 1, 0:n_past, :] = ck[:, qk_dim:]
                vt_scr[hh, 0:v_dim, 0:n_past] = cv_ref[:, head_lanes(hh)].T.astype(BF16)
            k_scr[2 * hh, n_past:, :] = kb[:, :qk_dim]
            k_scr[2 * hh + 1, n_past:, :] = kb[:, qk_dim:]
            vt_scr[hh, 0:v_dim, n_past:] = v_ref[:, head_lanes(hh)].astype(F32).T.astype(BF16)
            vt_scr[hh, v_dim:, :] = jnp.ones((ONES_ROWS, nk), BF16)

    nt = (((1,), (1,)), ((), ()))
    qh = []
    for hh in range(n_heads):
        qb = q_ref[:, head_lanes(hh)]
        if n_past:
            qb = _rope(qb.astype(F32), cq_ref[...], sq_ref[...]).astype(BF16)
        qh += [qb[:, :qk_dim], qb[:, qk_dim:]]
    n_maps = 2 * n_heads
    m = [jnp.full((1, tq), -jnp.inf, F32)] * n_maps
    acc = [jnp.zeros((v_dim + ONES_ROWS, tq), F32)] * n_maps
    n_items = n_maps * (nk // key_chunk)
    scores = {}
    for t in range(n_items + SCORE_LEAD):
        if t < n_items:
            c, a = divmod(t, n_maps)
            scores[t] = lax.dot_general(k_scr[a, c * key_chunk:(c + 1) * key_chunk, :], qh[a], nt,
                                        preferred_element_type=F32)
        if t >= SCORE_LEAD:
            c, a = divmod(t - SCORE_LEAD, n_maps)
            s = scores.pop(t - SCORE_LEAD)
            m_new = jnp.maximum(m[a], jnp.max(s, axis=0, keepdims=True))
            p = jnp.exp2(s - m_new).astype(BF16)
            acc[a] = jnp.exp2(m[a] - m_new) * acc[a] + jnp.dot(
                vt_scr[a // 2, :, c * key_chunk:(c + 1) * key_chunk], p, preferred_element_type=F32)
            m[a] = m_new
    for hh in range(n_heads):
        a0, a1 = acc[2 * hh], acc[2 * hh + 1]
        l0 = a0[v_dim:v_dim + 1, :]
        l1 = a1[v_dim:v_dim + 1, :]
        o = a0[:v_dim, :] * (1.0 / l0) - a1[:v_dim, :] * (lam_ref[0] / l1)
        o = o * lax.rsqrt(jnp.mean(o * o, axis=0, keepdims=True) + EPS) * (hg_ref[...] * out_scale)
        o_ref[:, head_lanes(hh)] = o.T.astype(o_ref.dtype)


def _attention(z, lam, head_g, layer, batch, seq, n_heads, qk_dim, v_dim, past=None, rope_tabs=None):
    assert 2 * qk_dim == LANES and v_dim == LANES
    t = batch * seq
    tq = _tile(seq, 256)
    nq = seq // tq
    n_past = past[0].shape[2] if past is not None else 0
    nk = n_past + seq
    hp = HEADS_PER_STEP if n_heads % HEADS_PER_STEP == 0 else 1
    hw = hp * LANES
    n_hb = n_heads // hp
    q_spec = pl.BlockSpec((tq, hw), lambda b, h, i: (b * nq + i, h))
    k_spec = pl.BlockSpec((seq, hw), lambda b, h, i: (b, n_hb + h))
    v_spec = pl.BlockSpec((seq, hw), lambda b, h, i: (b, 2 * n_hb + h))
    smem = pl.BlockSpec(memory_space=pltpu.SMEM)
    hg_spec = pl.BlockSpec((LANES, 1), lambda b, h, i: (0, 0))
    if n_past:
        cache_spec = pl.BlockSpec((None, None, n_past, hw), lambda b, h, i: (b, layer, 0, h))
        tq_tab = pl.BlockSpec((tq, LANES), lambda b, h, i: (i, 0))
        seq_tab = pl.BlockSpec((seq, LANES), lambda b, h, i: (0, 0))
        cos, sin = rope_tabs
        in_specs = [smem, q_spec, k_spec, v_spec, cache_spec, cache_spec, tq_tab, tq_tab, seq_tab, seq_tab, hg_spec]
        args = (lam, z, z, z, past[0], past[1], cos, sin, cos, sin, head_g.reshape(LANES, 1))
    else:
        in_specs = [smem, q_spec, k_spec, v_spec, hg_spec]
        args = (lam, z, z, z, head_g.reshape(LANES, 1))
    return pl.pallas_call(
        functools.partial(_attn_kernel, n_past=n_past, qk_dim=qk_dim,
                          out_scale=1.0 - _lambda_init(layer), key_chunk=_tile(nk, 256)),
        out_shape=jax.ShapeDtypeStruct((t, n_heads * v_dim), BF16),
        grid=(batch, n_hb, nq),
        in_specs=in_specs,
        out_specs=pl.BlockSpec((tq, hw), lambda b, h, i: (b * nq + i, h)),
        scratch_shapes=[pltpu.VMEM((2 * hp, nk, qk_dim), BF16), pltpu.VMEM((hp, v_dim + ONES_ROWS, nk), BF16)],
        compiler_params=_params("arbitrary", "arbitrary", "arbitrary"),
        name="diff_attn",
    )(*args)


def _rope_tables(seq, qk_dim):
    half = qk_dim // 2
    inv = 1.0 / (ROPE_BASE ** (jnp.arange(0, half, 2, dtype=F32) / half))
    pos = jnp.arange(seq)
    sign = jnp.concatenate([-jnp.ones((half // 2,), F32), jnp.ones((half // 2,), F32)])

    def tab(p):
        ang = p.astype(F32)[:, None] * inv[None, :]
        cos = jnp.concatenate([jnp.cos(ang), jnp.cos(ang)], -1)
        sin = jnp.concatenate([jnp.sin(ang), jnp.sin(ang)], -1) * sign[None, :]
        return cos, sin

    cr, sr = tab(pos // GRID_W)
    cc, sc = tab(pos % GRID_W)
    cos = jnp.concatenate([cr, cc], -1)
    sin = jnp.concatenate([sr, sc], -1)
    reps = LANES // qk_dim
    return jnp.tile(cos, (1, reps)), jnp.tile(sin, (1, reps))


def _ssm_kernel(x_ref, m_ref, bre_ref, bim_ref, cre_ref, cim_ref, are_ref, aim_ref, h0re_ref, h0im_ref,
                y_ref, fre_ref, fim_ref, sre_scr, sim_scr, hfre_scr, hfim_scr, hbre_scr, hbim_scr,
                *, n_chunks, batch, state):
    xb = x_ref[...].astype(BF16)
    sre_scr[...] = jnp.dot(xb, bre_ref[...], preferred_element_type=F32)
    sim_scr[...] = jnp.dot(xb, bim_ref[...], preferred_element_type=F32)
    ar = are_ref[...]
    ai = aim_ref[...]
    fwd_lane = lax.broadcasted_iota(jnp.int32, (batch, 2 * state), 1) < state

    def step(i, carry):
        hre, him = carry
        rf = pl.ds(i, batch, stride=n_chunks)
        rb = pl.ds(n_chunks - 1 - i, batch, stride=n_chunks)
        hfre_scr[rf, :] = hre
        hfim_scr[rf, :] = him
        hbre_scr[rb, :] = hre
        hbim_scr[rb, :] = him
        sre = jnp.where(fwd_lane, sre_scr[rf, :], sre_scr[rb, :])
        sim = jnp.where(fwd_lane, sim_scr[rf, :], sim_scr[rb, :])
        return ar * hre - ai * him + sre, ar * him + ai * hre + sim

    hre, him = lax.fori_loop(0, n_chunks, step, (h0re_ref[...], h0im_ref[...]))
    fre_ref[...] = hre
    fim_ref[...] = him
    lane = lax.broadcasted_iota(jnp.int32, hfre_scr.shape, 1) < state
    hs_re = jnp.where(lane, hfre_scr[...], hbre_scr[...]).astype(BF16)
    hs_im = jnp.where(lane, hfim_scr[...], hbim_scr[...]).astype(BF16)
    y = jnp.dot(xb, m_ref[...], preferred_element_type=F32)
    y += jnp.dot(hs_re, cre_ref[...], preferred_element_type=F32)
    y += jnp.dot(hs_im, cim_ref[...], preferred_element_type=F32)
    y_ref[...] = y


def _ssm_matrices(a_re, a_im, log_dt, b_re, b_im, c_re, c_im):
    tc = SSM_CHUNK
    hi = lax.Precision.HIGHEST
    g, p = a_re.shape[1:]
    ch = b_re.shape[-1]
    tau = jnp.arange(tc + 1, dtype=F32)
    s_idx = jnp.arange(tc)
    m_tot = 0.0
    b_parts, c_parts, a_parts = [], [], []
    for d in range(2):
        lam = lax.complex(a_re[d].astype(F32), a_im[d].astype(F32))
        lam_dt = lam * jnp.exp(log_dt[d].astype(F32))[:, None]
        a_bar = jnp.exp(lam_dt)
        b_bar = ((a_bar - 1.0) / lam)[..., None] * lax.complex(b_re[d].astype(F32), b_im[d].astype(F32))
        c_mat = lax.complex(c_re[d].astype(F32), c_im[d].astype(F32))
        pw = jnp.exp(tau[:, None, None] * lam_dt[None])
        kern = jnp.einsum('gcp,tgp,gpd->tgcd', c_mat, pw[:tc], b_bar, precision=hi).real
        lag = (s_idx[None, :] - s_idx[:, None]) if d == 0 else (s_idx[:, None] - s_idx[None, :])
        valid = (lag >= 0)
        kk = kern[jnp.clip(lag, 0, tc - 1)]
        kk = jnp.where(valid[:, :, None, None, None], kk, 0.0)
        m_tot = m_tot + jnp.transpose(kk, (2, 0, 4, 1, 3)).reshape(g, tc * ch, tc * ch)
        pw_in = pw[tc - 1 - s_idx] if d == 0 else pw[s_idx]
        bs = pw_in[:, :, :, None] * b_bar[None]
        b_parts.append(jnp.transpose(bs, (1, 0, 3, 2)).reshape(g, tc * ch, p))
        pw_out = pw[s_idx + 1] if d == 0 else pw[tc - s_idx]
        cs = c_mat[None] * pw_out[:, :, None, :]
        c_parts.append(jnp.transpose(cs, (1, 3, 0, 2)).reshape(g, p, tc * ch))
        a_parts.append(pw[tc])
    bcat = jnp.concatenate(b_parts, axis=-1)
    ccat = jnp.concatenate(c_parts, axis=1)
    acat = jnp.concatenate(a_parts, axis=-1)[:, None, :]
    return (m_tot.astype(BF16), bcat.real.astype(BF16), bcat.imag.astype(BF16),
            ccat.real.astype(BF16), (-ccat.imag).astype(BF16), acat.real, acat.imag)


def _chunk_rows(seq):
    nc = seq // SSM_CHUNK
    rb = _tile(nc, 128)
    return nc, rb, nc // rb


def _slot_transpose(arrs, ch):
    n = len(arrs)
    assert n * ch == LANES and n & (n - 1) == 0
    slot = lax.broadcasted_iota(jnp.int32, arrs[0].shape, 1) // ch
    arrs = list(arrs)
    s = n // 2
    while s:
        upper = (slot & s) != 0
        for lo in range(n):
            if lo & s:
                continue
            a, b = arrs[lo], arrs[lo + s]
            arrs[lo] = jnp.where(upper, pltpu.roll(b, s * ch, 1), a)
            arrs[lo + s] = jnp.where(upper, b, pltpu.roll(a, LANES - s * ch, 1))
        s //= 2
    return arrs


def _ssm_pack_kernel(*refs, rb, ch):
    u_refs, x_ref = refs[:-1], refs[-1]
    per_tile = LANES // ch
    for j, u_ref in enumerate(u_refs):
        for half in range(SSM_CHUNK // per_tile):
            rows_t = [u_ref[pl.ds(half * per_tile + tt, rb, stride=SSM_CHUNK), :] for tt in range(per_tile)]
            for gg, a in enumerate(_slot_transpose(rows_t, ch)):
                x_ref[j * per_tile + gg, :, half * LANES:(half + 1) * LANES] = a


def _ssm_pack(z, u_col_block, width, n_groups, batch, seq):
    ch = width // n_groups
    kc = SSM_CHUNK * ch
    assert LANES % ch == 0 and SSM_CHUNK % (LANES // ch) == 0 and kc % LANES == 0
    nc, rb, nbk = _chunk_rows(seq)
    n_tiles = width // LANES
    first_tile = u_col_block * n_tiles
    return pl.pallas_call(
        functools.partial(_ssm_pack_kernel, rb=rb, ch=ch),
        out_shape=jax.ShapeDtypeStruct((n_groups, batch * nc, kc), F32),
        grid=(batch * nbk,),
        in_specs=[pl.BlockSpec((rb * SSM_CHUNK, LANES), functools.partial(lambda i, col: (i, col), col=first_tile + j))
                  for j in range(n_tiles)],
        out_specs=pl.BlockSpec((n_groups, rb, kc), lambda i: (0, i, 0)),
        compiler_params=_params("arbitrary"),
        name="s5_pack",
    )(*([z] * n_tiles))


def _ssm(x, mats, h0re, h0im, batch, seq):
    m, bre, bim, cre, cim, are, aim = mats
    g, kc, _ = m.shape
    p2 = bre.shape[-1]
    nc = seq // SSM_CHUNK
    n = nc * batch
    grp = lambda shape: pl.BlockSpec((None,) + shape, lambda i: (i,) + (0,) * len(shape))
    y, fre, fim = pl.pallas_call(
        functools.partial(_ssm_kernel, n_chunks=nc, batch=batch, state=p2 // 2),
        out_shape=(jax.ShapeDtypeStruct((g, n, kc), F32),
                   jax.ShapeDtypeStruct((g, batch, p2), F32),
                   jax.ShapeDtypeStruct((g, batch, p2), F32)),
        grid=(g,),
        in_specs=[grp((n, kc)), grp((kc, kc)), grp((kc, p2)), grp((kc, p2)), grp((p2, kc)), grp((p2, kc)),
                  grp((1, p2)), grp((1, p2)), grp((batch, p2)), grp((batch, p2))],
        out_specs=(grp((n, kc)), grp((batch, p2)), grp((batch, p2))),
        scratch_shapes=[pltpu.VMEM((n, p2), F32)] * 6,
        compiler_params=_params("arbitrary"),
        name="s5_scan",
    )(x, m, bre, bim, cre, cim, are, aim, h0re, h0im)
    return y, fre, fim


def _ssm_post_kernel(y_ref, u_ref, d_ref, w_ref, g_ref, o_ref, y_scr, *, rb, ch):
    per_tile = LANES // ch
    n_tiles = y_scr.shape[0]
    for j in range(n_tiles):
        for half in range(SSM_CHUNK // per_tile):
            groups = [y_ref[j * per_tile + gg, :, half * LANES:(half + 1) * LANES] for gg in range(per_tile)]
            for tt, a in enumerate(_slot_transpose(groups, ch)):
                y_scr[j, pl.ds(half * per_tile + tt, rb, stride=SSM_CHUNK), :] = a
    y = jnp.concatenate([y_scr[j] for j in range(n_tiles)], axis=1) + d_ref[...] * u_ref[...]
    y = 0.5 * y * (1.0 + jnp.tanh(math.sqrt(2.0 / math.pi) * (y + 0.044715 * (y * y * y))))
    y = y * _sigmoid(jnp.dot(y.astype(BF16), w_ref[...], preferred_element_type=F32))
    y = y * lax.rsqrt(jnp.mean(y * y, axis=-1, keepdims=True) + EPS) * g_ref[...]
    o_ref[...] = y.astype(o_ref.dtype)


def _ssm_post(y, z, u_col_block, ssm_d, w_glu, norm_g, batch, seq):
    g, _, kc = y.shape
    w = w_glu.shape[0]
    nc, rb, nbk = _chunk_rows(seq)
    tm = rb * SSM_CHUNK
    fixed = lambda i: (0, 0)
    return pl.pallas_call(
        functools.partial(_ssm_post_kernel, rb=rb, ch=w // g),
        out_shape=jax.ShapeDtypeStruct((batch * seq, w), BF16),
        grid=(batch * nbk,),
        in_specs=[pl.BlockSpec((g, rb, kc), lambda i: (0, i, 0)),
                  pl.BlockSpec((tm, w), lambda i: (i, u_col_block)),
                  pl.BlockSpec((1, w), fixed), pl.BlockSpec((w, w), fixed), pl.BlockSpec((1, w), fixed)],
        out_specs=pl.BlockSpec((tm, w), lambda i: (i, 0)),
        scratch_shapes=[pltpu.VMEM((w // LANES, tm, LANES), F32)],
        compiler_params=_params("arbitrary"),
        name="s5_post",
    )(y, z, ssm_d.reshape(1, w), w_glu, norm_g.reshape(1, w))


def _out_proj_kernel(x_ref, a_ref, s_ref, wa_ref, ws_ref, mod_ref, o_ref):
    f = jnp.dot(a_ref[...], wa_ref[...], preferred_element_type=F32)
    f += jnp.dot(s_ref[...], ws_ref[...], preferred_element_type=F32)
    o_ref[...] = x_ref[...] + mod_ref[2:3, :] * f


def _out_proj(x, attn, ssm, w_attn, w_ssm, mod, rows_per_mod):
    t, d = x.shape
    ka, ks = attn.shape[1], ssm.shape[1]
    tm = _tile(rows_per_mod, 1024)
    tn = _tile(d, 1024)
    tiles_per_mod = rows_per_mod // tm
    return pl.pallas_call(
        _out_proj_kernel,
        out_shape=jax.ShapeDtypeStruct((t, d), F32),
        grid=(t // tm, d // tn),
        in_specs=[
            pl.BlockSpec((tm, tn), lambda i, j: (i, j)),
            pl.BlockSpec((tm, ka), lambda i, j: (i, 0)),
            pl.BlockSpec((tm, ks), lambda i, j: (i, 0)),
            pl.BlockSpec((ka, tn), lambda i, j: (0, j)),
            pl.BlockSpec((ks, tn), lambda i, j: (0, j)),
            pl.BlockSpec((None, MOD_ROWS, tn), lambda i, j: (i // tiles_per_mod, 0, j)),
        ],
        out_specs=pl.BlockSpec((tm, tn), lambda i, j: (i, j)),
        compiler_params=_params("arbitrary", "arbitrary"),
        name="out_proj",
    )(x, attn, ssm, w_attn, w_ssm, mod)


def _swiglu_tile(h, w1_ref, w3_ref, w2_ref):
    tf = w1_ref.shape[1]
    piece = FFN_PIECE if tf % FFN_PIECE == 0 else tf
    cols = [slice(k, k + piece) for k in range(0, tf, piece)]

    def up(c):
        return (jnp.dot(h, w1_ref[:, c], preferred_element_type=F32),
                jnp.dot(h, w3_ref[:, c], preferred_element_type=F32))

    out = None
    ab = up(cols[0])
    for k, c in enumerate(cols):
        a, b = ab
        if k + 1 < len(cols):
            ab = up(cols[k + 1])
        g = (a * _sigmoid(a) * b).astype(BF16)
        y = jnp.dot(g, w2_ref[c, :], preferred_element_type=F32)
        out = y if out is None else out + y
    return out


def _ffn_kernel(x_ref, g_ref, mod_ref, w1_ref, w3_ref, w2_ref, o_ref, h_scr, acc_scr):
    f = pl.program_id(1)

    @pl.when(f == 0)
    def _():
        h_scr[...] = _norm_mod(x_ref[...], g_ref[...], mod_ref[...], 3, 4).astype(BF16)
        acc_scr[...] = jnp.zeros_like(acc_scr)

    acc_scr[...] += _swiglu_tile(h_scr[...], w1_ref, w3_ref, w2_ref)

    @pl.when(f == pl.num_programs(1) - 1)
    def _():
        o_ref[...] = x_ref[...] + mod_ref[5:6, :] * acc_scr[...]


def _ffn(x, g, mod, w1, w3, w2, rows_per_mod):
    t, d = x.shape
    dff = w1.shape[1]
    tm = _tile(rows_per_mod, 512)
    tf = _tile(dff, 512)
    tiles_per_mod = rows_per_mod // tm
    return pl.pallas_call(
        _ffn_kernel,
        out_shape=jax.ShapeDtypeStruct((t, d), F32),
        grid=(t // tm, dff // tf),
        in_specs=[
            pl.BlockSpec((tm, d), lambda i, f: (i, 0)),
            pl.BlockSpec((1, d), lambda i, f: (0, 0)),
            pl.BlockSpec((None, MOD_ROWS, d), lambda i, f: (i // tiles_per_mod, 0, 0)),
            pl.BlockSpec((d, tf), lambda i, f: (0, f)),
            pl.BlockSpec((d, tf), lambda i, f: (0, f)),
            pl.BlockSpec((tf, d), lambda i, f: (f, 0)),
        ],
        out_specs=pl.BlockSpec((tm, d), lambda i, f: (i, 0)),
        scratch_shapes=[pltpu.VMEM((tm, d), BF16), pltpu.VMEM((tm, d), F32)],
        compiler_params=_params("arbitrary", "arbitrary"),
        name="ffn_dense",
    )(x, g.reshape(1, d), mod, w1, w3, w2)


def _router_kernel(x_ref, g_ref, mod_ref, wr_ref, h_ref, r_ref, *, n_experts):
    h = _norm_mod(x_ref[...], g_ref[...], mod_ref[...], 3, 4)
    h_ref[...] = h
    logits = jnp.dot(h.astype(BF16), wr_ref[...], preferred_element_type=F32)
    lane = lax.broadcasted_iota(jnp.int32, logits.shape, 1)
    neg = jnp.float32(-jnp.inf)
    logits = jnp.where(lane < n_experts, logits, neg)
    m1 = jnp.max(logits, axis=-1, keepdims=True)
    i1 = jnp.min(jnp.where(logits == m1, lane, LANES), axis=-1, keepdims=True)
    rest = jnp.where(lane == i1, neg, logits)
    m2 = jnp.max(rest, axis=-1, keepdims=True)
    i2 = jnp.min(jnp.where(rest == m2, lane, LANES), axis=-1, keepdims=True)
    e2 = jnp.exp(m2 - m1)
    den = 1.0 + e2
    r = jnp.where(lane == 0, i1.astype(F32), 0.0)
    r = jnp.where(lane == 1, i2.astype(F32), r)
    r = jnp.where(lane == 2, 1.0 / den, r)
    r = jnp.where(lane == 3, e2 / den, r)
    r_ref[...] = r


def _router(x, g, mod, w_router, rows_per_mod):
    t, d = x.shape
    n_experts = w_router.shape[1]
    wr = jnp.zeros((d, LANES), BF16).at[:, :n_experts].set(w_router.astype(BF16))
    tm = _tile(rows_per_mod, 512)
    tiles_per_mod = rows_per_mod // tm
    return pl.pallas_call(
        functools.partial(_router_kernel, n_experts=n_experts),
        out_shape=(jax.ShapeDtypeStruct((t, d), F32), jax.ShapeDtypeStruct((t, LANES), F32)),
        grid=(t // tm,),
        in_specs=[
            pl.BlockSpec((tm, d), lambda i: (i, 0)),
            pl.BlockSpec((1, d), lambda i: (0, 0)),
            pl.BlockSpec((None, MOD_ROWS, d), lambda i: (i // tiles_per_mod, 0, 0)),
            pl.BlockSpec((d, LANES), lambda i: (0, 0)),
        ],
        out_specs=(pl.BlockSpec((tm, d), lambda i: (i, 0)), pl.BlockSpec((tm, LANES), lambda i: (i, 0))),
        compiler_params=_params("arbitrary"),
        name="moe_router",
    )(x, g.reshape(1, d), mod, wr)


GATHER_UNROLL = 8


def _row_copy(idx_ref, src_hbm, dst_ref, sem, r):
    return pltpu.make_async_copy(src_hbm.at[pl.ds(idx_ref[r], 1), :], dst_ref.at[pl.ds(r, 1), :], sem)


def _gather_start(idx_ref, src_hbm, dst_ref, sem, n_rows, first=0):
    def body(r, c):
        _row_copy(idx_ref, src_hbm, dst_ref, sem, first + r).start()
        return c

    lax.fori_loop(0, n_rows, body, 0, unroll=GATHER_UNROLL)


def _gather_wait(idx_ref, src_hbm, dst_ref, sem, n_rows):
    def body(r, c):
        _row_copy(idx_ref, src_hbm, dst_ref, sem, r).wait()
        return c

    lax.fori_loop(0, n_rows, body, 0, unroll=GATHER_UNROLL)


def _moe_kernel(be_ref, nu_ref, tok_ref, tok_next_ref, h_hbm, w1_ref, w3_ref, w2_ref, o_ref,
                xs_scr, hb_scr, acc_scr, sem, *, rows, gather_steps):
    i = pl.program_id(0)
    f = pl.program_id(1)
    n_used = nu_ref[0]
    used = i < n_used

    @pl.when(jnp.logical_and(used, jnp.logical_and(i == 0, f == 0)))
    def _():
        _gather_start(tok_ref.at[0, 0], h_hbm, xs_scr, sem, rows)

    @pl.when(jnp.logical_and(used, f == 0))
    def _():
        _gather_wait(tok_ref.at[0, 0], h_hbm, xs_scr, sem, rows)
        hb_scr[...] = xs_scr[...].astype(BF16)
        acc_scr[...] = jnp.zeros_like(acc_scr)

    @pl.when(jnp.logical_and(i + 1 < n_used, jnp.logical_and(f >= 1, f <= gather_steps)))
    def _():
        part = rows // gather_steps
        _gather_start(tok_next_ref.at[0, 0], h_hbm, xs_scr, sem, part, first=(f - 1) * part)

    @pl.when(used)
    def _():
        acc_scr[...] += _swiglu_tile(hb_scr[...], w1_ref, w3_ref, w2_ref)

    @pl.when(f == pl.num_programs(1) - 1)
    def _():
        @pl.when(used)
        def _():
            o_ref[...] = acc_scr[...]

        @pl.when(jnp.logical_not(used))
        def _():
            o_ref[...] = jnp.zeros_like(o_ref)


def _moe_ffn(h, slot_tok, block_e, n_used, w1, w3, w2):
    d = h.shape[1]
    dff = w1.shape[2]
    n_slots = slot_tok.shape[0]
    rows = MOE_ROWS
    nb = n_slots // rows
    tf = _tile(dff, 512)
    nf = dff // tf
    assert nf >= 2
    gather_steps = max(s for s in (1, 2, 4, 8) if s < nf and rows % (s * GATHER_UNROLL) == 0)

    def wcol(i, f, be, nu):
        return (be[i], 0, jnp.where(i < nu[0], f, nf - 1))

    def wrow(i, f, be, nu):
        return (be[i], jnp.where(i < nu[0], f, nf - 1), 0)

    tok = slot_tok.reshape(nb, 1, rows)
    return pl.pallas_call(
        functools.partial(_moe_kernel, rows=rows, gather_steps=gather_steps),
        out_shape=jax.ShapeDtypeStruct((n_slots, d), F32),
        grid_spec=pltpu.PrefetchScalarGridSpec(
            num_scalar_prefetch=2,
            grid=(nb, nf),
            in_specs=[
                pl.BlockSpec((1, 1, rows), lambda i, f, be, nu: (i, 0, 0), memory_space=pltpu.SMEM),
                pl.BlockSpec((1, 1, rows), lambda i, f, be, nu: (jnp.minimum(i + 1, nb - 1), 0, 0),
                             memory_space=pltpu.SMEM),
                pl.BlockSpec(memory_space=pl.ANY),
                pl.BlockSpec((None, d, tf), wcol),
                pl.BlockSpec((None, d, tf), wcol),
                pl.BlockSpec((None, tf, d), wrow),
            ],
            out_specs=pl.BlockSpec((rows, d), lambda i, f, be, nu: (i, 0)),
            scratch_shapes=[pltpu.VMEM((rows, d), F32), pltpu.VMEM((rows, d), BF16),
                            pltpu.VMEM((rows, d), F32), pltpu.SemaphoreType.DMA],
        ),
        compiler_params=_params("arbitrary", "arbitrary"),
        name="moe_ffn",
    )(block_e, n_used, tok, tok, h, w1, w3, w2)


def _moe_combine_kernel(s0_ref, s1_ref, s0n_ref, s1n_ref, r_ref, x_ref, mod_ref, ys_hbm, o_ref, y_scr, sem,
                        *, rows):
    i = pl.program_id(0)
    cur = i % 2

    def start(idx0_ref, idx1_ref, buf):
        _gather_start(idx0_ref.at[0, 0], ys_hbm, y_scr.at[buf, 0], sem.at[buf], rows)
        _gather_start(idx1_ref.at[0, 0], ys_hbm, y_scr.at[buf, 1], sem.at[buf], rows)

    @pl.when(i == 0)
    def _():
        start(s0_ref, s1_ref, 0)

    @pl.when(i + 1 < pl.num_programs(0))
    def _():
        start(s0n_ref, s1n_ref, 1 - cur)

    _gather_wait(s0_ref.at[0, 0], ys_hbm, y_scr.at[cur, 0], sem.at[cur], rows)
    _gather_wait(s1_ref.at[0, 0], ys_hbm, y_scr.at[cur, 1], sem.at[cur], rows)
    r = r_ref[...]
    f = y_scr[cur, 0] * r[:, TOP_K:TOP_K + 1] + y_scr[cur, 1] * r[:, TOP_K + 1:TOP_K + 2]
    o_ref[...] = x_ref[...] + mod_ref[5:6, :] * f


def _moe_combine(x, mod, ys, route, slot0, slot1, rows_per_mod):
    t, d = x.shape
    tm = _tile(rows_per_mod, 256)
    nt = t // tm
    tiles_per_mod = rows_per_mod // tm
    cur_spec = pl.BlockSpec((1, 1, tm), lambda i: (i, 0, 0), memory_space=pltpu.SMEM)
    next_spec = pl.BlockSpec((1, 1, tm), lambda i: (jnp.minimum(i + 1, nt - 1), 0, 0), memory_space=pltpu.SMEM)
    s0 = slot0.reshape(nt, 1, tm)
    s1 = slot1.reshape(nt, 1, tm)
    return pl.pallas_call(
        functools.partial(_moe_combine_kernel, rows=tm),
        out_shape=jax.ShapeDtypeStruct((t, d), F32),
        grid=(nt,),
        in_specs=[
            cur_spec, cur_spec, next_spec, next_spec,
            pl.BlockSpec((tm, LANES), lambda i: (i, 0)),
            pl.BlockSpec((tm, d), lambda i: (i, 0)),
            pl.BlockSpec((None, MOD_ROWS, d), lambda i: (i // tiles_per_mod, 0, 0)),
            pl.BlockSpec(memory_space=pl.ANY),
        ],
        out_specs=pl.BlockSpec((tm, d), lambda i: (i, 0)),
        scratch_shapes=[pltpu.VMEM((2, TOP_K, tm, d), F32), pltpu.SemaphoreType.DMA((2,))],
        compiler_params=_params("arbitrary"),
        name="moe_combine",
    )(s0, s1, s0, s1, route, x, mod, ys)


def _moe(x, g, mod, w_router, w1, w3, w2, rows_per_mod):
    t, d = x.shape
    n_experts = w_router.shape[1]
    h, r = _router(x, g, mod, w_router, rows_per_mod)
    e_flat = r[:, :TOP_K].astype(jnp.int32).reshape(-1)
    n_assign = t * TOP_K
    onehot = (e_flat[:, None] == jnp.arange(n_experts, dtype=jnp.int32)[None, :]).astype(jnp.int32)
    csum = jnp.cumsum(onehot, axis=0)
    rank = jnp.take_along_axis(csum, e_flat[:, None], axis=1)[:, 0] - 1
    counts = csum[-1]
    padded = (counts + MOE_ROWS - 1) // MOE_ROWS * MOE_ROWS
    pad_end = jnp.cumsum(padded)
    pad_start = pad_end - padded
    slot = (pad_start[e_flat] + rank).astype(jnp.int32)
    nb = -(-n_assign // MOE_ROWS) + n_experts
    n_slots = nb * MOE_ROWS
    tok = jnp.arange(n_assign, dtype=jnp.int32) // TOP_K
    slot_tok = jnp.zeros((n_slots,), jnp.int32).at[slot].set(tok, unique_indices=True)
    block_start = jnp.arange(nb, dtype=jnp.int32) * MOE_ROWS
    block_e = jnp.minimum(jnp.sum(block_start[:, None] >= pad_end[None, :], axis=1), n_experts - 1).astype(jnp.int32)
    n_used = (pad_end[-1:] // MOE_ROWS).astype(jnp.int32)
    ys = _moe_ffn(h, slot_tok, block_e, n_used, w1, w3, w2)
    slot2 = slot.reshape(t, TOP_K)
    return _moe_combine(x, mod, ys, r, slot2[:, 0], slot2[:, 1], rows_per_mod)


def _final_norm_kernel(x_ref, g_ref, o_ref):
    x = x_ref[...]
    o_ref[...] = x * lax.rsqrt(jnp.mean(x * x, axis=-1, keepdims=True) + EPS) * g_ref[...]


def _final_norm(x, g):
    t, d = x.shape
    tm = _tile(t, 1024)
    return pl.pallas_call(
        _final_norm_kernel,
        out_shape=jax.ShapeDtypeStruct((t, d), F32),
        grid=(t // tm,),
        in_specs=[pl.BlockSpec((tm, d), lambda i: (i, 0)), pl.BlockSpec((1, d), lambda i: (0, 0))],
        out_specs=pl.BlockSpec((tm, d), lambda i: (i, 0)),
        compiler_params=_params("arbitrary"),
        name="final_norm",
    )(x, g.reshape(1, d))


def kernel(x_prompt, x_sample, cache_k, cache_v, state_ssm_re, state_ssm_im, c, c_ctx, w_mod, b_mod, norm1_g, norm2_g, w_in, w_out, lam_q1, lam_k1, lam_q2, lam_k2, head_g, ssm_a_re, ssm_a_im, ssm_log_dt, ssm_b_re, ssm_b_im, ssm_c_re, ssm_c_im, ssm_d, w_glu, ssm_norm_g, ffn_w1, ffn_w3, ffn_w2, moe_router, moe_w1, moe_w3, moe_w2, final_g):
    batch, seq, d = x_prompt.shape
    dec_batch, dec_seq, _ = x_sample.shape
    depth = w_mod.shape[0]
    _, _, n_past, n_heads, _, qk_dim = cache_k.shape
    v_dim = cache_v.shape[-1]
    n_groups, n_state = state_ssm_re.shape[-2:]
    ssm_w = w_glu.shape[-1]
    qk_w = n_heads * 2 * qk_dim
    attn_w = n_heads * v_dim
    u_off = 2 * qk_w + attn_w
    assert u_off % ssm_w == 0 and ssm_w % LANES == 0

    n_cond = 1 + dec_batch
    cond = jnp.zeros((-(-n_cond // 8) * 8, d), F32).at[0].set(c_ctx).at[1:n_cond].set(c)
    mods = _adaln(cond, w_mod, b_mod)[:, :n_cond].reshape(depth, n_cond, N_MOD, d)
    mods = jnp.pad(mods, ((0, 0), (0, 0), (0, MOD_ROWS - N_MOD), (0, 0)))

    cos, sin = _rope_tables(dec_seq, qk_dim)
    ck = cache_k.reshape(dec_batch, depth, n_past, qk_w)
    cv = cache_v.reshape(dec_batch, depth, n_past, attn_w)

    def to_groups(s):
        return jnp.transpose(s, (2, 0, 1, 3)).reshape(n_groups, s.shape[0], 2 * n_state)

    xp = x_prompt.reshape(batch * seq, d)
    xs = x_sample.reshape(dec_batch * dec_seq, d)
    zeros_state = jnp.zeros((n_groups, batch, 2 * n_state), F32)
    q_scale = qk_dim ** -0.5 * math.log2(math.e)
    new_k = jnp.zeros((batch, depth, seq, qk_w), F32)
    new_v = jnp.zeros((batch, depth, seq, attn_w), F32)
    new_re, new_im = [], []
    for l in range(depth):
        w_in_l = jnp.concatenate([w_in[l, :, u_off:], w_in[l, :, :qk_w] * q_scale, w_in[l, :, qk_w:u_off]],
                                 axis=1).astype(BF16)
        w_out_a = w_out[l, :attn_w].astype(BF16)
        w_out_s = w_out[l, attn_w:].astype(BF16)
        w_glu_l = w_glu[l].astype(BF16)
        lam = (jnp.exp(jnp.sum(lam_q1[l] * lam_k1[l])) - jnp.exp(jnp.sum(lam_q2[l] * lam_k2[l]))
               + _lambda_init(l)).reshape(1).astype(F32)
        mats = _ssm_matrices(ssm_a_re[l], ssm_a_im[l], ssm_log_dt[l], ssm_b_re[l], ssm_b_im[l],
                             ssm_c_re[l], ssm_c_im[l])
        if l % 2 == 0:
            ffn_w = (ffn_w1[l // 2].astype(BF16), ffn_w3[l // 2].astype(BF16), ffn_w2[l // 2].astype(BF16))
        else:
            ffn_w = (moe_w1[l // 2].astype(BF16), moe_w3[l // 2].astype(BF16), moe_w2[l // 2].astype(BF16))

        def layer(x, mod, b, s, rows_per_mod, past, tabs, kv_out, h0re, h0im):
            z, u, *kv = _in_proj(x, norm1_g[l], mod, w_in_l, rows_per_mod, qk_w, attn_w, ssm_w, kv_out)
            attn = _attention(z, lam, head_g[l], l, b, s, n_heads, qk_dim, v_dim, past, tabs)
            y, fre, fim = _ssm(_ssm_pack(u, 0, ssm_w, n_groups, b, s), mats, h0re, h0im, b, s)
            ssm = _ssm_post(y, u, 0, ssm_d[l], w_glu_l, ssm_norm_g[l], b, s)
            x = _out_proj(x, attn, ssm, w_out_a, w_out_s, mod, rows_per_mod)
            if l % 2 == 0:
                x = _ffn(x, norm2_g[l], mod, *ffn_w, rows_per_mod)
            else:
                x = _moe(x, norm2_g[l], mod, moe_router[l // 2], *ffn_w, rows_per_mod)
            return x, kv, fre, fim

        xp, (new_k, new_v), fre, fim = layer(xp, mods[l, :1], batch, seq, batch * seq, None, None,
                                             (l, seq, new_k, new_v), zeros_state, zeros_state)
        new_re.append(jnp.transpose(fre.reshape(n_groups, batch, 2, n_state), (1, 2, 0, 3)))
        new_im.append(jnp.transpose(fim.reshape(n_groups, batch, 2, n_state), (1, 2, 0, 3)))
        xs, _, _, _ = layer(xs, mods[l, 1:], dec_batch, dec_seq, dec_seq, (ck, cv), (cos, sin), None,
                            to_groups(state_ssm_re[:, l]), to_groups(state_ssm_im[:, l]))

    y_prompt = _final_norm(xp, final_g).reshape(batch, seq, d)
    y_sample = _final_norm(xs, final_g).reshape(dec_batch, dec_seq, d)
    return (y_prompt, y_sample, new_k.reshape(batch, depth, seq, n_heads, 2, qk_dim),
            new_v.reshape(batch, depth, seq, n_heads, v_dim), jnp.stack(new_re, axis=1), jnp.stack(new_im, axis=1))
```

```python
import functools
import math

import jax
import jax.numpy as jnp
from jax import lax
from jax.experimental import pallas as pl
from jax.experimental.pallas import tpu as pltpu

F32 = jnp.float32
BF16 = jnp.bfloat16

EPS = 1e-6
GRID_W = 64
ROPE_BASE = 10000.0
TOP_K = 2
SSM_CHUNK = 16
MOE_ROWS = 512
N_MOD = 6
MOD_ROWS = 8
LANES = 128
FFN_PIECE = 256
SCORE_LEAD = 5
VMEM_LIMIT_BYTES = 56 * 1024 * 1024


def _lambda_init(l):
    return 0.8 - 0.6 * math.exp(-0.3 * l)


def _tile(dim, pref):
    if dim <= pref:
        return dim
    t = pref
    while dim % t:
        t -= 8
    assert t > 0
    return t


def _params(*sem):
    return pltpu.CompilerParams(dimension_semantics=sem, vmem_limit_bytes=VMEM_LIMIT_BYTES)


def _sigmoid(x):
    return 1.0 / (1.0 + jnp.exp(-x))


def _norm_mod(x, g, mod, shift_row, scale_row):
    y = x * lax.rsqrt(jnp.mean(x * x, axis=-1, keepdims=True) + EPS) * g
    return y * (1.0 + mod[scale_row:scale_row + 1, :]) + mod[shift_row:shift_row + 1, :]


def _adaln_kernel(c_ref, w_ref, b_ref, o_ref):
    c = c_ref[...]
    s = (c * _sigmoid(c)).astype(BF16)
    o_ref[...] = jnp.dot(s, w_ref[...].astype(BF16), preferred_element_type=F32) + b_ref[...]


def _adaln(cond, w_mod, b_mod):
    depth, d, n = w_mod.shape
    r = cond.shape[0]
    tn = _tile(n, 1024)
    return pl.pallas_call(
        _adaln_kernel,
        out_shape=jax.ShapeDtypeStruct((depth, r, n), F32),
        grid=(depth, n // tn),
        in_specs=[
            pl.BlockSpec((r, d), lambda l, j: (0, 0)),
            pl.BlockSpec((None, d, tn), lambda l, j: (l, 0, j)),
            pl.BlockSpec((None, 1, tn), lambda l, j: (l, 0, j)),
        ],
        out_specs=pl.BlockSpec((None, r, tn), lambda l, j: (l, 0, j)),
        compiler_params=_params("arbitrary", "arbitrary"),
        name="adaln",
    )(cond, w_mod, b_mod.reshape(depth, 1, n))


def _rope(x, cos, sin_signed):
    width = x.shape[1]
    lane = lax.broadcasted_iota(jnp.int32, x.shape, 1)
    first_half = (lane % 32) < 16
    partner = jnp.where(first_half, pltpu.roll(x, width - 16, 1), pltpu.roll(x, 16, 1))
    return x * cos + partner * sin_signed


def _in_proj_kernel(*refs, with_kv, n_alias, q_tiles, kv_tiles):
    x_ref, g_ref, mod_ref, w_ref = refs[:4]
    refs = refs[4 + n_alias:]
    if with_kv:
        z_ref, u_ref, k_ref, v_ref, h_scr = refs
    else:
        z_ref, u_ref, h_scr = refs
    j = pl.program_id(1)

    @pl.when(j == 0)
    def _():
        h_scr[...] = _norm_mod(x_ref[...], g_ref[...], mod_ref[...], 0, 1).astype(BF16)

    r = jnp.dot(h_scr[...], w_ref[...], preferred_element_type=F32)
    z_ref[...] = r.astype(BF16)

    @pl.when(j == 0)
    def _():
        u_ref[...] = r

    if with_kv:
        @pl.when(jnp.logical_and(j > q_tiles, j <= q_tiles + kv_tiles))
        def _():
            k_ref[...] = r.reshape(k_ref.shape)

        @pl.when(j > q_tiles + kv_tiles)
        def _():
            v_ref[...] = r.reshape(v_ref.shape)


def _in_proj(x, g, mod, w, rows_per_mod, qk_w, attn_w, ssm_w, kv_out=None):
    t, d = x.shape
    tm = _tile(rows_per_mod, 1024)
    tn = ssm_w
    assert qk_w == attn_w and qk_w % tn == 0
    q_tiles = kv_tiles = qk_w // tn
    qkv_tiles = q_tiles + 2 * kv_tiles
    tiles_per_mod = rows_per_mod // tm
    in_specs = [
        pl.BlockSpec((tm, d), lambda i, j: (i, 0)),
        pl.BlockSpec((1, d), lambda i, j: (0, 0)),
        pl.BlockSpec((None, MOD_ROWS, d), lambda i, j: (i // tiles_per_mod, 0, 0)),
        pl.BlockSpec((d, tn), lambda i, j: (0, j)),
    ]
    args = [x, g.reshape(1, d), mod, w]
    out_shape = [jax.ShapeDtypeStruct((t, qkv_tiles * tn), BF16), jax.ShapeDtypeStruct((t, ssm_w), F32)]
    out_specs = [pl.BlockSpec((tm, tn), lambda i, j: (i, jnp.maximum(j - 1, 0))),
                 pl.BlockSpec((tm, tn), lambda i, j: (i, 0))]
    aliases = {}
    if kv_out is not None:
        layer, seq, new_k, new_v = kv_out
        assert tm % seq == 0
        bt = tm // seq
        aliases = {len(args): 2, len(args) + 1: 3}
        in_specs += [pl.BlockSpec(memory_space=pl.ANY)] * 2
        args += [new_k, new_v]
        out_shape += [jax.ShapeDtypeStruct(a.shape, F32) for a in (new_k, new_v)]
        out_specs += [
            pl.BlockSpec((bt, None, seq, tn),
                         lambda i, j: (i, layer, 0, jnp.clip(j - 1 - q_tiles, 0, kv_tiles - 1))),
            pl.BlockSpec((bt, None, seq, tn),
                         lambda i, j: (i, layer, 0, jnp.clip(j - 1 - q_tiles - kv_tiles, 0, kv_tiles - 1))),
        ]
    return pl.pallas_call(
        functools.partial(_in_proj_kernel, with_kv=kv_out is not None, n_alias=len(aliases),
                          q_tiles=q_tiles, kv_tiles=kv_tiles),
        out_shape=tuple(out_shape),
        grid=(t // tm, 1 + qkv_tiles),
        in_specs=in_specs,
        out_specs=tuple(out_specs),
        scratch_shapes=[pltpu.VMEM((tm, d), BF16)],
        input_output_aliases=aliases,
        compiler_params=_params("arbitrary", "arbitrary"),
        name="in_proj",
    )(*args)


ONES_ROWS = 16
HEADS_PER_STEP = 4


def _attn_kernel(*refs, n_past, qk_dim, out_scale, key_chunk):
    if n_past:
        (lam_ref, q_ref, k_ref, v_ref, ck_ref, cv_ref, cq_ref, sq_ref, ckk_ref, skk_ref, hg_ref,
         o_ref, k_scr, vt_scr) = refs
    else:
        lam_ref, q_ref, k_ref, v_ref, hg_ref, o_ref, k_scr, vt_scr = refs
    tq = q_ref.shape[0]
    n_heads = q_ref.shape[1] // LANES
    nk = vt_scr.shape[2]
    v_dim = vt_scr.shape[1] - ONES_ROWS
    head_lanes = lambda hh: slice(hh * LANES, (hh + 1) * LANES)

    @pl.when(pl.program_id(2) == 0)
    def _():
        for hh in range(n_heads):
            kb = k_ref[:, head_lanes(hh)]
            if n_past:
                kb = _rope(kb.astype(F32), ckk_ref[...], skk_ref[...]).astype(BF16)
                ck = ck_ref[:, head_lanes(hh)].astype(BF16)
                k_scr[2 * hh, 0:n_past, :] = ck[:, :qk_dim]
                k_scr[2 * hh + 1, 0:n_past, :] = ck[:, qk_dim:]
                vt_scr[hh, 0:v_dim, 0:n_past] = cv_ref[:, head_lanes(hh)].T.astype(BF16)
            k_scr[2 * hh, n_past:, :] = kb[:, :qk_dim]
            k_scr[2 * hh + 1, n_past:, :] = kb[:, qk_dim:]
            vt_scr[hh, 0:v_dim, n_past:] = v_ref[:, head_lanes(hh)].astype(F32).T.astype(BF16)
            vt_scr[hh, v_dim:, :] = jnp.ones((ONES_ROWS, nk), BF16)

    nt = (((1,), (1,)), ((), ()))
    qh = []
    for hh in range(n_heads):
        qb = q_ref[:, head_lanes(hh)]
        if n_past:
            qb = _rope(qb.astype(F32), cq_ref[...], sq_ref[...]).astype(BF16)
        qh += [qb[:, :qk_dim], qb[:, qk_dim:]]
    n_maps = 2 * n_heads
    m = [jnp.full((1, tq), -jnp.inf, F32)] * n_maps
    acc = [jnp.zeros((v_dim + ONES_ROWS, tq), F32)] * n_maps
    n_items = n_maps * (nk // key_chunk)
    scores = {}
    for t in range(n_items + SCORE_LEAD):
        if t < n_items:
            c, a = divmod(t, n_maps)
            scores[t] = lax.dot_general(k_scr[a, c * key_chunk:(c + 1) * key_chunk, :], qh[a], nt,
                                        preferred_element_type=F32)
        if t >= SCORE_LEAD:
            c, a = divmod(t - SCORE_LEAD, n_maps)
            s = scores.pop(t - SCORE_LEAD)
            m_new = jnp.maximum(m[a], jnp.max(s, axis=0, keepdims=True))
            p = jnp.exp2(s - m_new).astype(BF16)
            acc[a] = jnp.exp2(m[a] - m_new) * acc[a] + jnp.dot(
                vt_scr[a // 2, :, c * key_chunk:(c + 1) * key_chunk], p, preferred_element_type=F32)
            m[a] = m_new
    for hh in range(n_heads):
        a0, a1 = acc[2 * hh], acc[2 * hh + 1]
        l0 = a0[v_dim:v_dim + 1, :]
        l1 = a1[v_dim:v_dim + 1, :]
        o = a0[:v_dim, :] * (1.0 / l0) - a1[:v_dim, :] * (lam_ref[0] / l1)
        o = o * lax.rsqrt(jnp.mean(o * o, axis=0, keepdims=True) + EPS) * (hg_ref[...] * out_scale)
        o_ref[:, head_lanes(hh)] = o.T.astype(o_ref.dtype)


def _attention(z, lam, head_g, layer, batch, seq, n_heads, qk_dim, v_dim, past=None, rope_tabs=None):
    assert 2 * qk_dim == LANES and v_dim == LANES
    t = batch * seq
    tq = _tile(seq, 256)
    nq = seq // tq
    n_past = past[0].shape[2] if past is not None else 0
    nk = n_past + seq
    hp = HEADS_PER_STEP if n_heads % HEADS_PER_STEP == 0 else 1
    hw = hp * LANES
    n_hb = n_heads // hp
    q_spec = pl.BlockSpec((tq, hw), lambda b, h, i: (b * nq + i, h))
    k_spec = pl.BlockSpec((seq, hw), lambda b, h, i: (b, n_hb + h))
    v_spec = pl.BlockSpec((seq, hw), lambda b, h, i: (b, 2 * n_hb + h))
    smem = pl.BlockSpec(memory_space=pltpu.SMEM)
    hg_spec = pl.BlockSpec((LANES, 1), lambda b, h, i: (0, 0))
    if n_past:
        cache_spec = pl.BlockSpec((None, None, n_past, hw), lambda b, h, i: (b, layer, 0, h))
        tq_tab = pl.BlockSpec((tq, LANES), lambda b, h, i: (i, 0))
        seq_tab = pl.BlockSpec((seq, LANES), lambda b, h, i: (0, 0))
        cos, sin = rope_tabs
        in_specs = [smem, q_spec, k_spec, v_spec, cache_spec, cache_spec, tq_tab, tq_tab, seq_tab, seq_tab, hg_spec]
        args = (lam, z, z, z, past[0], past[1], cos, sin, cos, sin, head_g.reshape(LANES, 1))
    else:
        in_specs = [smem, q_spec, k_spec, v_spec, hg_spec]
        args = (lam, z, z, z, head_g.reshape(LANES, 1))
    return pl.pallas_call(
        functools.partial(_attn_kernel, n_past=n_past, qk_dim=qk_dim,
                          out_scale=1.0 - _lambda_init(layer), key_chunk=_tile(nk, 256)),
        out_shape=jax.ShapeDtypeStruct((t, n_heads * v_dim), BF16),
        grid=(batch, n_hb, nq),
        in_specs=in_specs,
        out_specs=pl.BlockSpec((tq, hw), lambda b, h, i: (b * nq + i, h)),
        scratch_shapes=[pltpu.VMEM((2 * hp, nk, qk_dim), BF16), pltpu.VMEM((hp, v_dim + ONES_ROWS, nk), BF16)],
        compiler_params=_params("arbitrary", "arbitrary", "arbitrary"),
        name="diff_attn",
    )(*args)


def _rope_tables(seq, qk_dim):
    half = qk_dim // 2
    inv = 1.0 / (ROPE_BASE ** (jnp.arange(0, half, 2, dtype=F32) / half))
    pos = jnp.arange(seq)
    sign = jnp.concatenate([-jnp.ones((half // 2,), F32), jnp.ones((half // 2,), F32)])

    def tab(p):
        ang = p.astype(F32)[:, None] * inv[None, :]
        cos = jnp.concatenate([jnp.cos(ang), jnp.cos(ang)], -1)
        sin = jnp.concatenate([jnp.sin(ang), jnp.sin(ang)], -1) * sign[None, :]
        return cos, sin

    cr, sr = tab(pos // GRID_W)
    cc, sc = tab(pos % GRID_W)
    cos = jnp.concatenate([cr, cc], -1)
    sin = jnp.concatenate([sr, sc], -1)
    reps = LANES // qk_dim
    return jnp.tile(cos, (1, reps)), jnp.tile(sin, (1, reps))


def _ssm_kernel(x_ref, m_ref, bre_ref, bim_ref, cre_ref, cim_ref, are_ref, aim_ref, h0re_ref, h0im_ref,
                y_ref, fre_ref, fim_ref, sre_scr, sim_scr, hfre_scr, hfim_scr, hbre_scr, hbim_scr,
                *, n_chunks, batch, state):
    xb = x_ref[...].astype(BF16)
    sre_scr[...] = jnp.dot(xb, bre_ref[...], preferred_element_type=F32)
    sim_scr[...] = jnp.dot(xb, bim_ref[...], preferred_element_type=F32)
    ar = are_ref[...]
    ai = aim_ref[...]
    fwd_lane = lax.broadcasted_iota(jnp.int32, (batch, 2 * state), 1) < state

    def step(i, carry):
        hre, him = carry
        rf = pl.ds(i, batch, stride=n_chunks)
        rb = pl.ds(n_chunks - 1 - i, batch, stride=n_chunks)
        hfre_scr[rf, :] = hre
        hfim_scr[rf, :] = him
        hbre_scr[rb, :] = hre
        hbim_scr[rb, :] = him
        sre = jnp.where(fwd_lane, sre_scr[rf, :], sre_scr[rb, :])
        sim = jnp.where(fwd_lane, sim_scr[rf, :], sim_scr[rb, :])
        return ar * hre - ai * him + sre, ar * him + ai * hre + sim

    hre, him = lax.fori_loop(0, n_chunks, step, (h0re_ref[...], h0im_ref[...]))
    fre_ref[...] = hre
    fim_ref[...] = him
    lane = lax.broadcasted_iota(jnp.int32, hfre_scr.shape, 1) < state
    hs_re = jnp.where(lane, hfre_scr[...], hbre_scr[...]).astype(BF16)
    hs_im = jnp.where(lane, hfim_scr[...], hbim_scr[...]).astype(BF16)
    y = jnp.dot(xb, m_ref[...], preferred_element_type=F32)
    y += jnp.dot(hs_re, cre_ref[...], preferred_element_type=F32)
    y += jnp.dot(hs_im, cim_ref[...], preferred_element_type=F32)
    y_ref[...] = y


def _ssm_matrices(a_re, a_im, log_dt, b_re, b_im, c_re, c_im):
    tc = SSM_CHUNK
    hi = lax.Precision.HIGHEST
    g, p = a_re.shape[1:]
    ch = b_re.shape[-1]
    tau = jnp.arange(tc + 1, dtype=F32)
    s_idx = jnp.arange(tc)
    m_tot = 0.0
    b_parts, c_parts, a_parts = [], [], []
    for d in range(2):
        lam = lax.complex(a_re[d].astype(F32), a_im[d].astype(F32))
        lam_dt = lam * jnp.exp(log_dt[d].astype(F32))[:, None]
        a_bar = jnp.exp(lam_dt)
        b_bar = ((a_bar - 1.0) / lam)[..., None] * lax.complex(b_re[d].astype(F32), b_im[d].astype(F32))
        c_mat = lax.complex(c_re[d].astype(F32), c_im[d].astype(F32))
        pw = jnp.exp(tau[:, None, None] * lam_dt[None])
        kern = jnp.einsum('gcp,tgp,gpd->tgcd', c_mat, pw[:tc], b_bar, precision=hi).real
        lag = (s_idx[None, :] - s_idx[:, None]) if d == 0 else (s_idx[:, None] - s_idx[None, :])
        valid = (lag >= 0)
        kk = kern[jnp.clip(lag, 0, tc - 1)]
        kk = jnp.where(valid[:, :, None, None, None], kk, 0.0)
        m_tot = m_tot + jnp.transpose(kk, (2, 0, 4, 1, 3)).reshape(g, tc * ch, tc * ch)
        pw_in = pw[tc - 1 - s_idx] if d == 0 else pw[s_idx]
        bs = pw_in[:, :, :, None] * b_bar[None]
        b_parts.append(jnp.transpose(bs, (1, 0, 3, 2)).reshape(g, tc * ch, p))
        pw_out = pw[s_idx + 1] if d == 0 else pw[tc - s_idx]
        cs = c_mat[None] * pw_out[:, :, None, :]
        c_parts.append(jnp.transpose(cs, (1, 3, 0, 2)).reshape(g, p, tc * ch))
        a_parts.append(pw[tc])
    bcat = jnp.concatenate(b_parts, axis=-1)
    ccat = jnp.concatenate(c_parts, axis=1)
    acat = jnp.concatenate(a_parts, axis=-1)[:, None, :]
    return (m_tot.astype(BF16), bcat.real.astype(BF16), bcat.imag.astype(BF16),
            ccat.real.astype(BF16), (-ccat.imag).astype(BF16), acat.real, acat.imag)


def _chunk_rows(seq):
    nc = seq // SSM_CHUNK
    rb = _tile(nc, 128)
    return nc, rb, nc // rb


def _slot_transpose(arrs, ch):
    n = len(arrs)
    assert n * ch == LANES and n & (n - 1) == 0
    slot = lax.broadcasted_iota(jnp.int32, arrs[0].shape, 1) // ch
    arrs = list(arrs)
    s = n // 2
    while s:
        upper = (slot & s) != 0
        for lo in range(n):
            if lo & s:
                continue
            a, b = arrs[lo], arrs[lo + s]
            arrs[lo] = jnp.where(upper, pltpu.roll(b, s * ch, 1), a)
            arrs[lo + s] = jnp.where(upper, b, pltpu.roll(a, LANES - s * ch, 1))
        s //= 2
    return arrs


def _ssm_pack_kernel(*refs, rb, ch):
    u_refs, x_ref = refs[:-1], refs[-1]
    per_tile = LANES // ch
    for j, u_ref in enumerate(u_refs):
        for half in range(SSM_CHUNK // per_tile):
            rows_t = [u_ref[pl.ds(half * per_tile + tt, rb, stride=SSM_CHUNK), :] for tt in range(per_tile)]
            for gg, a in enumerate(_slot_transpose(rows_t, ch)):
                x_ref[j * per_tile + gg, :, half * LANES:(half + 1) * LANES] = a


def _ssm_pack(z, u_col_block, width, n_groups, batch, seq):
    ch = width // n_groups
    kc = SSM_CHUNK * ch
    assert LANES % ch == 0 and SSM_CHUNK % (LANES // ch) == 0 and kc % LANES == 0
    nc, rb, nbk = _chunk_rows(seq)
    n_tiles = width // LANES
    first_tile = u_col_block * n_tiles
    return pl.pallas_call(
        functools.partial(_ssm_pack_kernel, rb=rb, ch=ch),
        out_shape=jax.ShapeDtypeStruct((n_groups, batch * nc, kc), F32),
        grid=(batch * nbk,),
        in_specs=[pl.BlockSpec((rb * SSM_CHUNK, LANES), functools.partial(lambda i, col: (i, col), col=first_tile + j))
                  for j in range(n_tiles)],
        out_specs=pl.BlockSpec((n_groups, rb, kc), lambda i: (0, i, 0)),
        compiler_params=_params("arbitrary"),
        name="s5_pack",
    )(*([z] * n_tiles))


def _ssm(x, mats, h0re, h0im, batch, seq):
    m, bre, bim, cre, cim, are, aim = mats
    g, kc, _ = m.shape
    p2 = bre.shape[-1]
    nc = seq // SSM_CHUNK
    n = nc * batch
    grp = lambda shape: pl.BlockSpec((None,) + shape, lambda i: (i,) + (0,) * len(shape))
    y, fre, fim = pl.pallas_call(
        functools.partial(_ssm_kernel, n_chunks=nc, batch=batch, state=p2 // 2),
        out_shape=(jax.ShapeDtypeStruct((g, n, kc), F32),
                   jax.ShapeDtypeStruct((g, batch, p2), F32),
                   jax.ShapeDtypeStruct((g, batch, p2), F32)),
        grid=(g,),
        in_specs=[grp((n, kc)), grp((kc, kc)), grp((kc, p2)), grp((kc, p2)), grp((p2, kc)), grp((p2, kc)),
                  grp((1, p2)), grp((1, p2)), grp((batch, p2)), grp((batch, p2))],
        out_specs=(grp((n, kc)), grp((batch, p2)), grp((batch, p2))),
        scratch_shapes=[pltpu.VMEM((n, p2), F32)] * 6,
        compiler_params=_params("arbitrary"),
        name="s5_scan",
    )(x, m, bre, bim, cre, cim, are, aim, h0re, h0im)
    return y, fre, fim


def _ssm_post_kernel(y_ref, u_ref, d_ref, w_ref, g_ref, o_ref, y_scr, *, rb, ch):
    per_tile = LANES // ch
    n_tiles = y_scr.shape[0]
    for j in range(n_tiles):
        for half in range(SSM_CHUNK // per_tile):
            groups = [y_ref[j * per_tile + gg, :, half * LANES:(half + 1) * LANES] for gg in range(per_tile)]
            for tt, a in enumerate(_slot_transpose(groups, ch)):
                y_scr[j, pl.ds(half * per_tile + tt, rb, stride=SSM_CHUNK), :] = a
    y = jnp.concatenate([y_scr[j] for j in range(n_tiles)], axis=1) + d_ref[...] * u_ref[...]
    y = 0.5 * y * (1.0 + jnp.tanh(math.sqrt(2.0 / math.pi) * (y + 0.044715 * (y * y * y))))
    y = y * _sigmoid(jnp.dot(y.astype(BF16), w_ref[...], preferred_element_type=F32))
    y = y * lax.rsqrt(jnp.mean(y * y, axis=-1, keepdims=True) + EPS) * g_ref[...]
    o_ref[...] = y.astype(o_ref.dtype)


def _ssm_post(y, z, u_col_block, ssm_d, w_glu, norm_g, batch, seq):
    g, _, kc = y.shape
    w = w_glu.shape[0]
    nc, rb, nbk = _chunk_rows(seq)
    tm = rb * SSM_CHUNK
    fixed = lambda i: (0, 0)
    return pl.pallas_call(
        functools.partial(_ssm_post_kernel, rb=rb, ch=w // g),
        out_shape=jax.ShapeDtypeStruct((batch * seq, w), BF16),
        grid=(batch * nbk,),
        in_specs=[pl.BlockSpec((g, rb, kc), lambda i: (0, i, 0)),
                  pl.BlockSpec((tm, w), lambda i: (i, u_col_block)),
                  pl.BlockSpec((1, w), fixed), pl.BlockSpec((w, w), fixed), pl.BlockSpec((1, w), fixed)],
        out_specs=pl.BlockSpec((tm, w), lambda i: (i, 0)),
        scratch_shapes=[pltpu.VMEM((w // LANES, tm, LANES), F32)],
        compiler_params=_params("arbitrary"),
        name="s5_post",
    )(y, z, ssm_d.reshape(1, w), w_glu, norm_g.reshape(1, w))


def _out_proj_kernel(x_ref, a_ref, s_ref, wa_ref, ws_ref, mod_ref, o_ref):
    f = jnp.dot(a_ref[...], wa_ref[...], preferred_element_type=F32)
    f += jnp.dot(s_ref[...], ws_ref[...], preferred_element_type=F32)
    o_ref[...] = x_ref[...] + mod_ref[2:3, :] * f


def _out_proj(x, attn, ssm, w_attn, w_ssm, mod, rows_per_mod):
    t, d = x.shape
    ka, ks = attn.shape[1], ssm.shape[1]
    tm = _tile(rows_per_mod, 1024)
    tn = _tile(d, 1024)
    tiles_per_mod = rows_per_mod // tm
    return pl.pallas_call(
        _out_proj_kernel,
        out_shape=jax.ShapeDtypeStruct((t, d), F32),
        grid=(t // tm, d // tn),
        in_specs=[
            pl.BlockSpec((tm, tn), lambda i, j: (i, j)),
            pl.BlockSpec((tm, ka), lambda i, j: (i, 0)),
            pl.BlockSpec((tm, ks), lambda i, j: (i, 0)),
            pl.BlockSpec((ka, tn), lambda i, j: (0, j)),
            pl.BlockSpec((ks, tn), lambda i, j: (0, j)),
            pl.BlockSpec((None, MOD_ROWS, tn), lambda i, j: (i // tiles_per_mod, 0, j)),
        ],
        out_specs=pl.BlockSpec((tm, tn), lambda i, j: (i, j)),
        compiler_params=_params("arbitrary", "arbitrary"),
        name="out_proj",
    )(x, attn, ssm, w_attn, w_ssm, mod)


def _swiglu_tile(h, w1_ref, w3_ref, w2_ref):
    tf = w1_ref.shape[1]
    piece = FFN_PIECE if tf % FFN_PIECE == 0 else tf
    cols = [slice(k, k + piece) for k in range(0, tf, piece)]

    def up(c):
        return (jnp.dot(h, w1_ref[:, c], preferred_element_type=F32),
                jnp.dot(h, w3_ref[:, c], preferred_element_type=F32))

    out = None
    ab = up(cols[0])
    for k, c in enumerate(cols):
        a, b = ab
        if k + 1 < len(cols):
            ab = up(cols[k + 1])
        g = (a * _sigmoid(a) * b).astype(BF16)
        y = jnp.dot(g, w2_ref[c, :], preferred_element_type=F32)
        out = y if out is None else out + y
    return out


def _ffn_kernel(x_ref, g_ref, mod_ref, w1_ref, w3_ref, w2_ref, o_ref, h_scr, acc_scr):
    f = pl.program_id(1)

    @pl.when(f == 0)
    def _():
        h_scr[...] = _norm_mod(x_ref[...], g_ref[...], mod_ref[...], 3, 4).astype(BF16)
        acc_scr[...] = jnp.zeros_like(acc_scr)

    acc_scr[...] += _swiglu_tile(h_scr[...], w1_ref, w3_ref, w2_ref)

    @pl.when(f == pl.num_programs(1) - 1)
    def _():
        o_ref[...] = x_ref[...] + mod_ref[5:6, :] * acc_scr[...]


def _ffn(x, g, mod, w1, w3, w2, rows_per_mod):
    t, d = x.shape
    dff = w1.shape[1]
    tm = _tile(rows_per_mod, 512)
    tf = _tile(dff, 512)
    tiles_per_mod = rows_per_mod // tm
    return pl.pallas_call(
        _ffn_kernel,
        out_shape=jax.ShapeDtypeStruct((t, d), F32),
        grid=(t // tm, dff // tf),
        in_specs=[
            pl.BlockSpec((tm, d), lambda i, f: (i, 0)),
            pl.BlockSpec((1, d), lambda i, f: (0, 0)),
            pl.BlockSpec((None, MOD_ROWS, d), lambda i, f: (i // tiles_per_mod, 0, 0)),
            pl.BlockSpec((d, tf), lambda i, f: (0, f)),
            pl.BlockSpec((d, tf), lambda i, f: (0, f)),
            pl.BlockSpec((tf, d), lambda i, f: (f, 0)),
        ],
        out_specs=pl.BlockSpec((tm, d), lambda i, f: (i, 0)),
        scratch_shapes=[pltpu.VMEM((tm, d), BF16), pltpu.VMEM((tm, d), F32)],
        compiler_params=_params("arbitrary", "arbitrary"),
        name="ffn_dense",
    )(x, g.reshape(1, d), mod, w1, w3, w2)


def _router_kernel(x_ref, g_ref, mod_ref, wr_ref, h_ref, r_ref, *, n_experts):
    h = _norm_mod(x_ref[...], g_ref[...], mod_ref[...], 3, 4)
    h_ref[...] = h
    logits = jnp.dot(h.astype(BF16), wr_ref[...], preferred_element_type=F32)
    lane = lax.broadcasted_iota(jnp.int32, logits.shape, 1)
    neg = jnp.float32(-jnp.inf)
    logits = jnp.where(lane < n_experts, logits, neg)
    m1 = jnp.max(logits, axis=-1, keepdims=True)
    i1 = jnp.min(jnp.where(logits == m1, lane, LANES), axis=-1, keepdims=True)
    rest = jnp.where(lane == i1, neg, logits)
    m2 = jnp.max(rest, axis=-1, keepdims=True)
    i2 = jnp.min(jnp.where(rest == m2, lane, LANES), axis=-1, keepdims=True)
    e2 = jnp.exp(m2 - m1)
    den = 1.0 + e2
    r = jnp.where(lane == 0, i1.astype(F32), 0.0)
    r = jnp.where(lane == 1, i2.astype(F32), r)
    r = jnp.where(lane == 2, 1.0 / den, r)
    r = jnp.where(lane == 3, e2 / den, r)
    r_ref[...] = r


def _router(x, g, mod, w_router, rows_per_mod):
    t, d = x.shape
    n_experts = w_router.shape[1]
    wr = jnp.zeros((d, LANES), BF16).at[:, :n_experts].set(w_router.astype(BF16))
    tm = _tile(rows_per_mod, 512)
    tiles_per_mod = rows_per_mod // tm
    return pl.pallas_call(
        functools.partial(_router_kernel, n_experts=n_experts),
        out_shape=(jax.ShapeDtypeStruct((t, d), F32), jax.ShapeDtypeStruct((t, LANES), F32)),
        grid=(t // tm,),
        in_specs=[
            pl.BlockSpec((tm, d), lambda i: (i, 0)),
            pl.BlockSpec((1, d), lambda i: (0, 0)),
            pl.BlockSpec((None, MOD_ROWS, d), lambda i: (i // tiles_per_mod, 0, 0)),
            pl.BlockSpec((d, LANES), lambda i: (0, 0)),
        ],
        out_specs=(pl.BlockSpec((tm, d), lambda i: (i, 0)), pl.BlockSpec((tm, LANES), lambda i: (i, 0))),
        compiler_params=_params("arbitrary"),
        name="moe_router",
    )(x, g.reshape(1, d), mod, wr)


GATHER_UNROLL = 8


def _row_copy(idx_ref, src_hbm, dst_ref, sem, r):
    return pltpu.make_async_copy(src_hbm.at[pl.ds(idx_ref[r], 1), :], dst_ref.at[pl.ds(r, 1), :], sem)


def _gather_start(idx_ref, src_hbm, dst_ref, sem, n_rows, first=0):
    def body(r, c):
        _row_copy(idx_ref, src_hbm, dst_ref, sem, first + r).start()
        return c

    lax.fori_loop(0, n_rows, body, 0, unroll=GATHER_UNROLL)


def _gather_wait(idx_ref, src_hbm, dst_ref, sem, n_rows):
    def body(r, c):
        _row_copy(idx_ref, src_hbm, dst_ref, sem, r).wait()
        return c

    lax.fori_loop(0, n_rows, body, 0, unroll=GATHER_UNROLL)


def _moe_kernel(be_ref, nu_ref, tok_ref, tok_next_ref, h_hbm, w1_ref, w3_ref, w2_ref, o_ref,
                xs_scr, hb_scr, acc_scr, sem, *, rows, gather_steps):
    i = pl.program_id(0)
    f = pl.program_id(1)
    n_used = nu_ref[0]
    used = i < n_used

    @pl.when(jnp.logical_and(used, jnp.logical_and(i == 0, f == 0)))
    def _():
        _gather_start(tok_ref.at[0, 0], h_hbm, xs_scr, sem, rows)

    @pl.when(jnp.logical_and(used, f == 0))
    def _():
        _gather_wait(tok_ref.at[0, 0], h_hbm, xs_scr, sem, rows)
        hb_scr[...] = xs_scr[...].astype(BF16)
        acc_scr[...] = jnp.zeros_like(acc_scr)

    @pl.when(jnp.logical_and(i + 1 < n_used, jnp.logical_and(f >= 1, f <= gather_steps)))
    def _():
        part = rows // gather_steps
        _gather_start(tok_next_ref.at[0, 0], h_hbm, xs_scr, sem, part, first=(f - 1) * part)

    @pl.when(used)
    def _():
        acc_scr[...] += _swiglu_tile(hb_scr[...], w1_ref, w3_ref, w2_ref)

    @pl.when(f == pl.num_programs(1) - 1)
    def _():
        @pl.when(used)
        def _():
            o_ref[...] = acc_scr[...]

        @pl.when(jnp.logical_not(used))
        def _():
            o_ref[...] = jnp.zeros_like(o_ref)


def _moe_ffn(h, slot_tok, block_e, n_used, w1, w3, w2):
    d = h.shape[1]
    dff = w1.shape[2]
    n_slots = slot_tok.shape[0]
    rows = MOE_ROWS
    nb = n_slots // rows
    tf = _tile(dff, 512)
    nf = dff // tf
    assert nf >= 2
    gather_steps = max(s for s in (1, 2, 4, 8) if s < nf and rows % (s * GATHER_UNROLL) == 0)

    def wcol(i, f, be, nu):
        return (be[i], 0, jnp.where(i < nu[0], f, nf - 1))

    def wrow(i, f, be, nu):
        return (be[i], jnp.where(i < nu[0], f, nf - 1), 0)

    tok = slot_tok.reshape(nb, 1, rows)
    return pl.pallas_call(
        functools.partial(_moe_kernel, rows=rows, gather_steps=gather_steps),
        out_shape=jax.ShapeDtypeStruct((n_slots, d), F32),
        grid_spec=pltpu.PrefetchScalarGridSpec(
            num_scalar_prefetch=2,
            grid=(nb, nf),
            in_specs=[
                pl.BlockSpec((1, 1, rows), lambda i, f, be, nu: (i, 0, 0), memory_space=pltpu.SMEM),
                pl.BlockSpec((1, 1, rows), lambda i, f, be, nu: (jnp.minimum(i + 1, nb - 1), 0, 0),
                             memory_space=pltpu.SMEM),
                pl.BlockSpec(memory_space=pl.ANY),
                pl.BlockSpec((None, d, tf), wcol),
                pl.BlockSpec((None, d, tf), wcol),
                pl.BlockSpec((None, tf, d), wrow),
            ],
            out_specs=pl.BlockSpec((rows, d), lambda i, f, be, nu: (i, 0)),
            scratch_shapes=[pltpu.VMEM((rows, d), F32), pltpu.VMEM((rows, d), BF16),
                            pltpu.VMEM((rows, d), F32), pltpu.SemaphoreType.DMA],
        ),
        compiler_params=_params("arbitrary", "arbitrary"),
        name="moe_ffn",
    )(block_e, n_used, tok, tok, h, w1, w3, w2)


def _moe_combine_kernel(s0_ref, s1_ref, s0n_ref, s1n_ref, r_ref, x_ref, mod_ref, ys_hbm, o_ref, y_scr, sem,
                        *, rows):
    i = pl.program_id(0)
    cur = i % 2

    def start(idx0_ref, idx1_ref, buf):
        _gather_start(idx0_ref.at[0, 0], ys_hbm, y_scr.at[buf, 0], sem.at[buf], rows)
        _gather_start(idx1_ref.at[0, 0], ys_hbm, y_scr.at[buf, 1], sem.at[buf], rows)

    @pl.when(i == 0)
    def _():
        start(s0_ref, s1_ref, 0)

    @pl.when(i + 1 < pl.num_programs(0))
    def _():
        start(s0n_ref, s1n_ref, 1 - cur)

    _gather_wait(s0_ref.at[0, 0], ys_hbm, y_scr.at[cur, 0], sem.at[cur], rows)
    _gather_wait(s1_ref.at[0, 0], ys_hbm, y_scr.at[cur, 1], sem.at[cur], rows)
    r = r_ref[...]
    f = y_scr[cur, 0] * r[:, TOP_K:TOP_K + 1] + y_scr[cur, 1] * r[:, TOP_K + 1:TOP_K + 2]
    o_ref[...] = x_ref[...] + mod_ref[5:6, :] * f


def _moe_combine(x, mod, ys, route, slot0, slot1, rows_per_mod):
    t, d = x.shape
    tm = _tile(rows_per_mod, 256)
    nt = t // tm
    tiles_per_mod = rows_per_mod // tm
    cur_spec = pl.BlockSpec((1, 1, tm), lambda i: (i, 0, 0), memory_space=pltpu.SMEM)
    next_spec = pl.BlockSpec((1, 1, tm), lambda i: (jnp.minimum(i + 1, nt - 1), 0, 0), memory_space=pltpu.SMEM)
    s0 = slot0.reshape(nt, 1, tm)
    s1 = slot1.reshape(nt, 1, tm)
    return pl.pallas_call(
        functools.partial(_moe_combine_kernel, rows=tm),
        out_shape=jax.ShapeDtypeStruct((t, d), F32),
        grid=(nt,),
        in_specs=[
            cur_spec, cur_spec, next_spec, next_spec,
            pl.BlockSpec((tm, LANES), lambda i: (i, 0)),
            pl.BlockSpec((tm, d), lambda i: (i, 0)),
            pl.BlockSpec((None, MOD_ROWS, d), lambda i: (i // tiles_per_mod, 0, 0)),
            pl.BlockSpec(memory_space=pl.ANY),
        ],
        out_specs=pl.BlockSpec((tm, d), lambda i: (i, 0)),
        scratch_shapes=[pltpu.VMEM((2, TOP_K, tm, d), F32), pltpu.SemaphoreType.DMA((2,))],
        compiler_params=_params("arbitrary"),
        name="moe_combine",
    )(s0, s1, s0, s1, route, x, mod, ys)


def _moe(x, g, mod, w_router, w1, w3, w2, rows_per_mod):
    t, d = x.shape
    n_experts = w_router.shape[1]
    h, r = _router(x, g, mod, w_router, rows_per_mod)
    e_flat = r[:, :TOP_K].astype(jnp.int32).reshape(-1)
    n_assign = t * TOP_K
    onehot = (e_flat[:, None] == jnp.arange(n_experts, dtype=jnp.int32)[None, :]).astype(jnp.int32)
    csum = jnp.cumsum(onehot, axis=0)
    rank = jnp.take_along_axis(csum, e_flat[:, None], axis=1)[:, 0] - 1
    counts = csum[-1]
    padded = (counts + MOE_ROWS - 1) // MOE_ROWS * MOE_ROWS
    pad_end = jnp.cumsum(padded)
    pad_start = pad_end - padded
    slot = (pad_start[e_flat] + rank).astype(jnp.int32)
    nb = -(-n_assign // MOE_ROWS) + n_experts
    n_slots = nb * MOE_ROWS
    tok = jnp.arange(n_assign, dtype=jnp.int32) // TOP_K
    slot_tok = jnp.zeros((n_slots,), jnp.int32).at[slot].set(tok, unique_indices=True)
    block_start = jnp.arange(nb, dtype=jnp.int32) * MOE_ROWS
    block_e = jnp.minimum(jnp.sum(block_start[:, None] >= pad_end[None, :], axis=1), n_experts - 1).astype(jnp.int32)
    n_used = (pad_end[-1:] // MOE_ROWS).astype(jnp.int32)
    ys = _moe_ffn(h, slot_tok, block_e, n_used, w1, w3, w2)
    slot2 = slot.reshape(t, TOP_K)
    return _moe_combine(x, mod, ys, r, slot2[:, 0], slot2[:, 1], rows_per_mod)


def _final_norm_kernel(x_ref, g_ref, o_ref):
    x = x_ref[...]
    o_ref[...] = x * lax.rsqrt(jnp.mean(x * x, axis=-1, keepdims=True) + EPS) * g_ref[...]


def _final_norm(x, g):
    t, d = x.shape
    tm = _tile(t, 1024)
    return pl.pallas_call(
        _final_norm_kernel,
        out_shape=jax.ShapeDtypeStruct((t, d), F32),
        grid=(t // tm,),
        in_specs=[pl.BlockSpec((tm, d), lambda i: (i, 0)), pl.BlockSpec((1, d), lambda i: (0, 0))],
        out_specs=pl.BlockSpec((tm, d), lambda i: (i, 0)),
        compiler_params=_params("arbitrary"),
        name="final_norm",
    )(x, g.reshape(1, d))


def kernel(x_prompt, x_sample, cache_k, cache_v, state_ssm_re, state_ssm_im, c, c_ctx, w_mod, b_mod, norm1_g, norm2_g, w_in, w_out, lam_q1, lam_k1, lam_q2, lam_k2, head_g, ssm_a_re, ssm_a_im, ssm_log_dt, ssm_b_re, ssm_b_im, ssm_c_re, ssm_c_im, ssm_d, w_glu, ssm_norm_g, ffn_w1, ffn_w3, ffn_w2, moe_router, moe_w1, moe_w3, moe_w2, final_g):
    batch, seq, d = x_prompt.shape
    dec_batch, dec_seq, _ = x_sample.shape
    depth = w_mod.shape[0]
    _, _, n_past, n_heads, _, qk_dim = cache_k.shape
    v_dim = cache_v.shape[-1]
    n_groups, n_state = state_ssm_re.shape[-2:]
    ssm_w = w_glu.shape[-1]
    qk_w = n_heads * 2 * qk_dim
    attn_w = n_heads * v_dim
    u_off = 2 * qk_w + attn_w
    assert u_off % ssm_w == 0 and ssm_w % LANES == 0

    n_cond = 1 + dec_batch
    cond = jnp.zeros((-(-n_cond // 8) * 8, d), F32).at[0].set(c_ctx).at[1:n_cond].set(c)
    mods = _adaln(cond, w_mod, b_mod)[:, :n_cond].reshape(depth, n_cond, N_MOD, d)
    mods = jnp.pad(mods, ((0, 0), (0, 0), (0, MOD_ROWS - N_MOD), (0, 0)))

    cos, sin = _rope_tables(dec_seq, qk_dim)
    ck = cache_k.reshape(dec_batch, depth, n_past, qk_w)
    cv = cache_v.reshape(dec_batch, depth, n_past, attn_w)

    def to_groups(s):
        return jnp.transpose(s, (2, 0, 1, 3)).reshape(n_groups, s.shape[0], 2 * n_state)

    xp = x_prompt.reshape(batch * seq, d)
    xs = x_sample.reshape(dec_batch * dec_seq, d)
    zeros_state = jnp.zeros((n_groups, batch, 2 * n_state), F32)
    q_scale = qk_dim ** -0.5 * math.log2(math.e)
    new_k = jnp.zeros((batch, depth, seq, qk_w), F32)
    new_v = jnp.zeros((batch, depth, seq, attn_w), F32)
    new_re, new_im = [], []
    for l in range(depth):
        w_in_l = jnp.concatenate([w_in[l, :, u_off:], w_in[l, :, :qk_w] * q_scale, w_in[l, :, qk_w:u_off]],
                                 axis=1).astype(BF16)
        w_out_a = w_out[l, :attn_w].astype(BF16)
        w_out_s = w_out[l, attn_w:].astype(BF16)
        w_glu_l = w_glu[l].astype(BF16)
        lam = (jnp.exp(jnp.sum(lam_q1[l] * lam_k1[l])) - jnp.exp(jnp.sum(lam_q2[l] * lam_k2[l]))
               + _lambda_init(l)).reshape(1).astype(F32)
        mats = _ssm_matrices(ssm_a_re[l], ssm_a_im[l], ssm_log_dt[l], ssm_b_re[l], ssm_b_im[l],
                             ssm_c_re[l], ssm_c_im[l])
        if l % 2 == 0:
            ffn_w = (ffn_w1[l // 2].astype(BF16), ffn_w3[l // 2].astype(BF16), ffn_w2[l // 2].astype(BF16))
        else:
            ffn_w = (moe_w1[l // 2].astype(BF16), moe_w3[l // 2].astype(BF16), moe_w2[l // 2].astype(BF16))

        def layer(x, mod, b, s, rows_per_mod, past, tabs, kv_out, h0re, h0im):
            z, u, *kv = _in_proj(x, norm1_g[l], mod, w_in_l, rows_per_mod, qk_w, attn_w, ssm_w, kv_out)
            attn = _attention(z, lam, head_g[l], l, b, s, n_heads, qk_dim, v_dim, past, tabs)
            y, fre, fim = _ssm(_ssm_pack(u, 0, ssm_w, n_groups, b, s), mats, h0re, h0im, b, s)
            ssm = _ssm_post(y, u, 0, ssm_d[l], w_glu_l, ssm_norm_g[l], b, s)
            x = _out_proj(x, attn, ssm, w_out_a, w_out_s, mod, rows_per_mod)
            if l % 2 == 0:
                x = _ffn(x, norm2_g[l], mod, *ffn_w, rows_per_mod)
            else:
                x = _moe(x, norm2_g[l], mod, moe_router[l // 2], *ffn_w, rows_per_mod)
            return x, kv, fre, fim

        xp, (new_k, new_v), fre, fim = layer(xp, mods[l, :1], batch, seq, batch * seq, None, None,
                                             (l, seq, new_k, new_v), zeros_state, zeros_state)
        new_re.append(jnp.transpose(fre.reshape(n_groups, batch, 2, n_state), (1, 2, 0, 3)))
        new_im.append(jnp.transpose(fim.reshape(n_groups, batch, 2, n_state), (1, 2, 0, 3)))
        xs, _, _, _ = layer(xs, mods[l, 1:], dec_batch, dec_seq, dec_seq, (ck, cv), (cos, sin), None,
                            to_groups(state_ssm_re[:, l]), to_groups(state_ssm_im[:, l]))

    y_prompt = _final_norm(xp, final_g).reshape(batch, seq, d)
    y_sample = _final_norm(xs, final_g).reshape(dec_batch, dec_seq, d)
    return (y_prompt, y_sample, new_k.reshape(batch, depth, seq, n_heads, 2, qk_dim),
            new_v.reshape(batch, depth, seq, n_heads, v_dim), jnp.stack(new_re, axis=1), jnp.stack(new_im, axis=1))
```
